```python
import jax, jax.numpy as jnp
from jax import lax
import numpy as np

D_MODEL = 1024
BATCH = 16
SEQ = 4096
DEPTH = 1

ATTN_HEADS = 8
ATTN_KV_HEADS = 2
ATTN_HEAD_DIM = 64
ATTN_GROUP = ATTN_HEADS // ATTN_KV_HEADS
WINDOW = 128
ATTN_BLOCK = 128
ROT_DIM = ATTN_HEAD_DIM // 4
ROPE_THETA = 500000.0
ATTN_Q_WIDTH = ATTN_HEADS * ATTN_HEAD_DIM
ATTN_KV_WIDTH = ATTN_KV_HEADS * ATTN_HEAD_DIM

DN_HEADS = 4
DN_KEY_DIM = 128
DN_VAL_DIM = 128
DN_CONV = 4
DN_CHUNK = 64
DN_K_WIDTH = DN_HEADS * DN_KEY_DIM
DN_V_WIDTH = DN_HEADS * DN_VAL_DIM
DN_CONV_DIM = 2 * DN_K_WIDTH + DN_V_WIDTH

MIX_WIDTH = ATTN_Q_WIDTH + DN_V_WIDTH

FFN_HIDDEN = ((-(-8 * D_MODEL // 3) + 255) // 256) * 256

NORM_EPS = 1e-6

IN_SPLIT_SIZES = (ATTN_Q_WIDTH, ATTN_KV_WIDTH, ATTN_KV_WIDTH, DN_CONV_DIM,
                  DN_HEADS, DN_HEADS, DN_V_WIDTH, D_MODEL, D_MODEL)
IN_WIDTH = sum(IN_SPLIT_SIZES)

kernel_name = "hybrid_swa_sink_gdn_swiglu_adaln"

F32 = jnp.float32


def rms_norm(x, gain):
    xf = x.astype(F32)
    y = xf * lax.rsqrt(jnp.mean(xf * xf, axis=-1, keepdims=True) + NORM_EPS)
    return (y * gain.astype(F32)).astype(x.dtype)


def l2_norm(x):
    return x * lax.rsqrt(jnp.sum(x * x, axis=-1, keepdims=True) + NORM_EPS)


def modulate(h, shift, scale):
    return h * (1 + scale[:, None, :]) + shift[:, None, :]


def partial_rope(x, cos, sin):
    half = ROT_DIM // 2
    x1 = x[..., :half].astype(F32)
    x2 = x[..., half:ROT_DIM].astype(F32)
    rot = jnp.concatenate([x1 * cos - x2 * sin, x2 * cos + x1 * sin], axis=-1)
    return jnp.concatenate([rot.astype(x.dtype), x[..., ROT_DIM:]], axis=-1)


def sliding_window_attention(q, k, v, sinks):
    B, S = q.shape[0], q.shape[1]
    nb = S // ATTN_BLOCK
    qb = q.astype(F32).reshape(B, nb, ATTN_BLOCK, ATTN_KV_HEADS, ATTN_GROUP, ATTN_HEAD_DIM)

    def with_prev(t):
        tb = t.astype(F32).reshape(B, nb, ATTN_BLOCK, ATTN_KV_HEADS, ATTN_HEAD_DIM)
        prev = jnp.pad(tb, ((0, 0), (1, 0), (0, 0), (0, 0), (0, 0)))[:, :-1]
        return jnp.concatenate([prev, tb], axis=2)

    kb = with_prev(k)
    vb = with_prev(v)
    s = jnp.einsum('bnqhgd,bnkhd->bnhgqk', qb, kb) * (ATTN_HEAD_DIM ** -0.5)
    qi = jnp.arange(ATTN_BLOCK)[:, None] + ATTN_BLOCK
    kj = jnp.arange(2 * ATTN_BLOCK)[None, :]
    dist = qi - kj
    band = (dist >= 0) & (dist < WINDOW)
    key_pos = jnp.arange(nb)[:, None] * ATTN_BLOCK + kj - ATTN_BLOCK
    valid = band[None] & (key_pos >= 0)[:, None, :]
    s = jnp.where(valid[None, :, None, None], s, -jnp.inf)
    sink = sinks.astype(F32).reshape(ATTN_KV_HEADS, ATTN_GROUP)[None, None, :, :, None, None]
    m = jnp.maximum(jnp.max(s, axis=-1, keepdims=True), sink)
    p = jnp.exp(s - m)
    probs = p / (jnp.sum(p, axis=-1, keepdims=True) + jnp.exp(sink - m))
    o = jnp.einsum('bnhgqk,bnkhd->bnqhgd', probs, vb)
    return o.reshape(B, S, ATTN_Q_WIDTH)


def causal_short_conv(x, w):
    y = lax.conv_general_dilated(x, w, window_strides=(1,), padding=[(DN_CONV - 1, 0)],
                                 dimension_numbers=('NWC', 'WIO', 'NWC'),
                                 feature_group_count=x.shape[-1])
    return jax.nn.silu(y)


def gated_delta_rule(q, k, v, g, beta):
    B, S, H, dk = q.shape
    dv = v.shape[-1]
    n = S // DN_CHUNK
    C = DN_CHUNK

    def chunk(t):
        return t.reshape(B, n, C, H, -1).transpose(0, 1, 3, 2, 4)

    q = chunk(q) * (dk ** -0.5)
    k = chunk(k)
    v = chunk(v)
    g = g.reshape(B, n, C, H).transpose(0, 1, 3, 2)
    beta = beta.reshape(B, n, C, H).transpose(0, 1, 3, 2)
    gc = jnp.cumsum(g, axis=-1)
    incl = jnp.tril(jnp.ones((C, C), dtype=bool))
    strict = jnp.tril(jnp.ones((C, C), dtype=bool), -1)
    decay = jnp.exp(jnp.where(incl, gc[..., :, None] - gc[..., None, :], -jnp.inf))
    kb = k * beta[..., None]
    L = jnp.where(strict, jnp.einsum('bnhid,bnhjd->bnhij', kb, k) * decay, 0.0)
    eye = jnp.eye(C, dtype=F32)
    T = lax.linalg.triangular_solve(L + eye, jnp.broadcast_to(eye, L.shape), left_side=True,
                                    lower=True, unit_diagonal=True)
    u = jnp.einsum('bnhij,bnhjd->bnhid', T, v * beta[..., None])
    w = jnp.einsum('bnhij,bnhjd->bnhid', T, kb * jnp.exp(gc)[..., None])
    a_intra = jnp.einsum('bnhid,bnhjd->bnhij', q, k) * decay
    q_dec = q * jnp.exp(gc)[..., None]
    g_last = gc[..., -1:]
    k_dec = k * jnp.exp(g_last - gc)[..., None]
    chunk_decay = jnp.exp(g_last[..., 0])

    def step(state, xs):
        u_c, w_c, q_c, k_c, a_c, d_c = xs
        v_new = u_c - jnp.einsum('bhcd,bhde->bhce', w_c, state)
        o_c = (jnp.einsum('bhcd,bhde->bhce', q_c, state)
               + jnp.einsum('bhij,bhje->bhie', a_c, v_new))
        state = state * d_c[..., None, None] + jnp.einsum('bhcd,bhce->bhde', k_c, v_new)
        return state, o_c

    xs = (jnp.moveaxis(u, 1, 0), jnp.moveaxis(w, 1, 0), jnp.moveaxis(q_dec, 1, 0),
          jnp.moveaxis(k_dec, 1, 0), jnp.moveaxis(a_intra, 1, 0), jnp.moveaxis(chunk_decay, 1, 0))
    state0 = jnp.zeros((B, H, dk, dv), dtype=F32)
    _, o = lax.scan(step, state0, xs)
    return o.transpose(1, 0, 3, 2, 4).reshape(B, S, H, dv)


def token_mixers(h, positions, w_in, conv_w, q_norm_g, k_norm_g, sinks, a_log, dt_bias,
                 dn_norm_g, w_branch, w_out):
    B, S, _ = h.shape
    split_idx = np.cumsum(IN_SPLIT_SIZES)[:-1].tolist()
    proj = h @ w_in
    aq, ak, av, dn_qkv, dn_b, dn_a, dn_z, gate_a, gate_d = jnp.split(proj, split_idx, axis=-1)

    aq = rms_norm(aq.reshape(B, S, ATTN_HEADS, ATTN_HEAD_DIM), q_norm_g)
    ak = rms_norm(ak.reshape(B, S, ATTN_KV_HEADS, ATTN_HEAD_DIM), k_norm_g)
    av = av.reshape(B, S, ATTN_KV_HEADS, ATTN_HEAD_DIM)
    inv_freq = ROPE_THETA ** (-jnp.arange(0, ROT_DIM, 2, dtype=F32) / ROT_DIM)
    ang = positions.astype(F32)[..., None] * inv_freq
    cos = jnp.cos(ang)[:, :, None, :]
    sin = jnp.sin(ang)[:, :, None, :]
    aq = partial_rope(aq, cos, sin)
    ak = partial_rope(ak, cos, sin)
    o_attn = sliding_window_attention(aq, ak, av, sinks)

    qkv = causal_short_conv(dn_qkv, conv_w)
    dq, dk_, dv_ = jnp.split(qkv, [DN_K_WIDTH, 2 * DN_K_WIDTH], axis=-1)
    dq = l2_norm(dq.reshape(B, S, DN_HEADS, DN_KEY_DIM).astype(F32))
    dk_ = l2_norm(dk_.reshape(B, S, DN_HEADS, DN_KEY_DIM).astype(F32))
    dv_ = dv_.reshape(B, S, DN_HEADS, DN_VAL_DIM).astype(F32)
    beta = jax.nn.sigmoid(dn_b.astype(F32))
    g = -jnp.exp(a_log.astype(F32)) * jax.nn.softplus(dn_a.astype(F32) + dt_bias.astype(F32))
    o_dn = gated_delta_rule(dq, dk_, dv_, g, beta)
    o_dn = rms_norm(o_dn, dn_norm_g) * jax.nn.silu(
        dn_z.astype(F32).reshape(B, S, DN_HEADS, DN_VAL_DIM))
    o_dn = o_dn.reshape(B, S, DN_V_WIDTH)

    y_attn = o_attn.astype(h.dtype) @ w_branch[:ATTN_Q_WIDTH]
    y_dn = o_dn.astype(h.dtype) @ w_branch[ATTN_Q_WIDTH:]
    merged = jax.nn.sigmoid(gate_a) * y_attn + jax.nn.sigmoid(gate_d) * y_dn
    return merged @ w_out


def swiglu(h, w_gate_up, w_down):
    gate, up = jnp.split(h @ w_gate_up, 2, axis=-1)
    return (jax.nn.silu(gate) * up) @ w_down


def setup_inputs(seed: int = 0) -> dict:
    key = jax.random.key(seed)
    ks = jax.random.split(key, 24)
    D = D_MODEL

    def nrm(k, shape, fan_in):
        return jax.random.normal(k, shape, F32) * (fan_in ** -0.5)

    def gain(k, shape):
        return 1.0 + 0.02 * jax.random.normal(k, shape, F32)

    x = jax.random.normal(ks[0], (BATCH, SEQ, D), F32)
    c = jax.random.normal(ks[1], (BATCH, D), F32)
    offset = jax.random.randint(ks[2], (BATCH, 1), 0, 4096, dtype=jnp.int32)
    positions = offset + jnp.arange(SEQ, dtype=jnp.int32)[None, :]
    dt = jnp.exp(jax.random.uniform(ks[13], (DEPTH, DN_HEADS), F32,
                                    jnp.log(0.001), jnp.log(0.1)))
    return {
        "x": x,
        "c": c,
        "positions": positions,
        "ada_w": nrm(ks[3], (DEPTH, D, 6 * D), D),
        "ada_b": 0.01 * jax.random.normal(ks[4], (DEPTH, 6 * D), F32),
        "norm1_g": gain(ks[5], (DEPTH, D)),
        "w_in": nrm(ks[6], (DEPTH, D, IN_WIDTH), D),
        "conv_w": nrm(ks[7], (DEPTH, DN_CONV, 1, DN_CONV_DIM), DN_CONV),
        "q_norm_g": gain(ks[8], (DEPTH, ATTN_HEAD_DIM)),
        "k_norm_g": gain(ks[9], (DEPTH, ATTN_HEAD_DIM)),
        "sinks": 0.5 * jax.random.normal(ks[10], (DEPTH, ATTN_HEADS), F32),
        "a_log": jnp.log(jax.random.uniform(ks[11], (DEPTH, DN_HEADS), F32, 1.0, 16.0)),
        "dt_bias": dt + jnp.log(-jnp.expm1(-dt)),
        "dn_norm_g": gain(ks[12], (DEPTH, DN_VAL_DIM)),
        "w_branch": nrm(ks[14], (DEPTH, MIX_WIDTH, D), ATTN_Q_WIDTH),
        "w_out": nrm(ks[15], (DEPTH, D, D), D),
        "norm2_g": gain(ks[16], (DEPTH, D)),
        "w_gate_up": nrm(ks[17], (DEPTH, D, 2 * FFN_HIDDEN), D),
        "w_down": nrm(ks[18], (DEPTH, FFN_HIDDEN, D), FFN_HIDDEN),
    }


def reference(x, c, positions, ada_w, ada_b, norm1_g, w_in, conv_w, q_norm_g, k_norm_g,
              sinks, a_log, dt_bias, dn_norm_g, w_branch, w_out, norm2_g, w_gate_up, w_down):
    cond = jax.nn.silu(c)
    for l in range(DEPTH):
        mod = cond @ ada_w[l] + ada_b[l]
        shift1, scale1, gate1, shift2, scale2, gate2 = jnp.split(mod, 6, axis=-1)
        h = modulate(rms_norm(x, norm1_g[l]), shift1, scale1)
        x = x + gate1[:, None, :] * token_mixers(
            h, positions, w_in[l], conv_w[l], q_norm_g[l], k_norm_g[l], sinks[l], a_log[l],
            dt_bias[l], dn_norm_g[l], w_branch[l], w_out[l])
        h = modulate(rms_norm(x, norm2_g[l]), shift2, scale2)
        x = x + gate2[:, None, :] * swiglu(h, w_gate_up[l], w_down[l])
    return x
```

```python
import functools

import jax
import jax.numpy as jnp
import numpy as np
from jax import lax
from jax.experimental import pallas as pl
from jax.experimental.pallas import tpu as pltpu

F32 = jnp.float32
BF16 = jnp.bfloat16

D_MODEL = 1024
ATTN_HEADS = 8
ATTN_KV_HEADS = 2
ATTN_HEAD_DIM = 64
ATTN_BLOCK = 128
ROT_DIM = ATTN_HEAD_DIM // 4
ROPE_THETA = 500000.0
ATTN_Q_WIDTH = ATTN_HEADS * ATTN_HEAD_DIM
ATTN_KV_WIDTH = ATTN_KV_HEADS * ATTN_HEAD_DIM
DN_HEADS = 4
DN_DIM = 128
DN_CONV = 4
DN_CHUNK = 64
DN_SUB = 8
DN_WIDTH = DN_HEADS * DN_DIM
DN_CONV_DIM = 3 * DN_WIDTH
FFN_HIDDEN = 2816
NORM_EPS = 1e-6
LANES = 128

COL_QKV = 0
COL_DN = COL_QKV + ATTN_Q_WIDTH + 2 * ATTN_KV_WIDTH
COL_BA = COL_DN + DN_CONV_DIM
COL_Z = COL_BA + LANES
COL_GATES = COL_Z + DN_WIDTH
IN_PACKED = COL_GATES + 2 * D_MODEL

VMEM_LIMIT = 56 * 1024 * 1024


def _sigmoid(x):
    return 1.0 / (1.0 + jnp.exp(-x))


def _dot(a, b):
    return jnp.dot(a, b, preferred_element_type=F32)


def _dot_nt(a, b):
    return lax.dot_general(a, b, (((1,), (1,)), ((), ())), preferred_element_type=F32)


def _const_spec(shape):
    nd = len(shape)
    return pl.BlockSpec(shape, lambda *_: (0,) * nd, pipeline_mode=pl.Buffered(1))


def _params(n_grid):
    return pltpu.CompilerParams(dimension_semantics=("arbitrary",) * n_grid,
                                vmem_limit_bytes=VMEM_LIMIT)


def _ada_kernel(c_ref, w_ref, b_ref, o_ref):
    c = c_ref[...]
    cond = c * _sigmoid(c)
    o_ref[...] = jnp.dot(cond, w_ref[...], preferred_element_type=F32,
                         precision=lax.Precision.HIGHEST) + b_ref[...]


def _ada(c, w, b):
    bsz, d = c.shape
    n = w.shape[1]
    bn = 1536
    return pl.pallas_call(
        _ada_kernel,
        grid=(n // bn,),
        in_specs=[pl.BlockSpec((bsz, d), lambda j: (0, 0)),
                  pl.BlockSpec((d, bn), lambda j: (0, j)),
                  pl.BlockSpec((1, bn), lambda j: (0, j))],
        out_specs=pl.BlockSpec((bsz, bn), lambda j: (0, j)),
        out_shape=jax.ShapeDtypeStruct((bsz, n), F32),
        compiler_params=_params(1),
        name="ada",
    )(c, w, b.reshape(1, n))


def _inproj_kernel(x_ref, mod_ref, pos_ref, g1_ref, w_ref, bd_ref, qkg_ref, freq_ref, convw_ref,
                   alog_ref, dtb_ref,
                   q_ref, k_ref, v_ref, dqkv_ref, bg_ref, zs_ref, sg_ref,
                   xc_ref):
    t = pl.program_id(1)
    tt = x_ref.shape[1]

    x = x_ref[0]
    ms = jnp.mean(x * x, axis=-1, keepdims=True)
    y = x * lax.rsqrt(ms + NORM_EPS) * g1_ref[...]
    shift1 = mod_ref[0, 0:1, :]
    scale1 = mod_ref[0, 1:2, :]
    h = (y * (1.0 + scale1) + shift1).astype(BF16)

    qkv = _dot(h, w_ref[:, COL_QKV:COL_DN])
    q = qkv[:, :ATTN_Q_WIDTH]
    k = qkv[:, ATTN_Q_WIDTH:ATTN_Q_WIDTH + ATTN_KV_WIDTH]
    v = qkv[:, ATTN_Q_WIDTH + ATTN_KV_WIDTH:]
    bd = bd_ref[...]
    ssq_q = _dot((q * q).astype(BF16), bd)
    ssq_k = _dot((k * k).astype(BF16), bd[:ATTN_KV_WIDTH, :ATTN_KV_WIDTH])
    qn = q * lax.rsqrt(ssq_q * (1.0 / ATTN_HEAD_DIM) + NORM_EPS) * qkg_ref[0:1, :]
    kn = k * lax.rsqrt(ssq_k * (1.0 / ATTN_HEAD_DIM) + NORM_EPS) * qkg_ref[1:2, :ATTN_KV_WIDTH]

    ang = pos_ref[0].astype(F32) * freq_ref[0:1, :]
    cos = jnp.cos(ang)
    sin = jnp.sin(ang)
    s_lo = sin * freq_ref[1:2, :]
    s_hi = sin * freq_ref[2:3, :]

    def rope(u):
        half = ROT_DIM // 2
        return (u * cos + pltpu.roll(u, LANES - half, axis=1) * s_lo
                + pltpu.roll(u, half, axis=1) * s_hi)

    for i in range(ATTN_Q_WIDTH // LANES):
        sl = slice(i * LANES, (i + 1) * LANES)
        q_ref[0, :, sl] = (rope(qn[:, sl]) * (ATTN_HEAD_DIM ** -0.5)).astype(BF16)
    k_ref[0] = rope(kn).astype(BF16)
    v_ref[0] = v.astype(BF16)

    @pl.when(t == 0)
    def _():
        xc_ref[0:8, :] = jnp.zeros((8, DN_CONV_DIM), F32)

    xc_ref[8:tt + 8, :] = _dot(h, w_ref[:, COL_DN:COL_BA])
    conv = convw_ref[3:4, :] * xc_ref[8:tt + 8, :]
    for i in range(DN_CONV - 1):
        off = 8 - (DN_CONV - 1) + i
        conv = conv + convw_ref[i:i + 1, :] * xc_ref[off:off + tt, :]
    xc_ref[0:8, :] = xc_ref[tt:tt + 8, :]
    act = conv * _sigmoid(conv)
    for i in range(2 * DN_HEADS):
        sl = slice(i * DN_DIM, (i + 1) * DN_DIM)
        u = act[:, sl]
        scale = DN_DIM ** -0.5 if i < DN_HEADS else 1.0
        un = u * (lax.rsqrt(jnp.sum(u * u, axis=-1, keepdims=True) + NORM_EPS) * scale)
        dqkv_ref[0, :, sl] = un.astype(BF16)
    dqkv_ref[0, :, 2 * DN_WIDTH:] = act[:, 2 * DN_WIDTH:].astype(BF16)

    ba = _dot(h, w_ref[:, COL_BA:COL_Z])
    sp_in = ba + dtb_ref[...]
    softplus = jnp.maximum(sp_in, 0.0) + jnp.log(1.0 + jnp.exp(-jnp.abs(sp_in)))
    g = -jnp.exp(alog_ref[...]) * softplus
    lane = lax.broadcasted_iota(jnp.int32, ba.shape, 1)
    bg_ref[0] = jnp.where(lane < DN_HEADS, _sigmoid(ba), g)

    z = _dot(h, w_ref[:, COL_Z:COL_GATES])
    zs_ref[0] = (z * _sigmoid(z)).astype(BF16)
    for i in range(2):
        sl = slice(COL_GATES + i * D_MODEL, COL_GATES + (i + 1) * D_MODEL)
        sg_ref[0, :, i * D_MODEL:(i + 1) * D_MODEL] = _sigmoid(_dot(h, w_ref[:, sl])).astype(BF16)


def _inproj(x, mod, pos, g1, w_packed, bd, qkg, freq, convw, alog, dtb, tt):
    bsz, seq, d = x.shape
    grid = (bsz, seq // tt)

    def tile(width):
        return pl.BlockSpec((1, tt, width), lambda b, t: (b, t, 0))

    out_shapes = [
        jax.ShapeDtypeStruct((bsz, seq, ATTN_Q_WIDTH), BF16),
        jax.ShapeDtypeStruct((bsz, seq, ATTN_KV_WIDTH), BF16),
        jax.ShapeDtypeStruct((bsz, seq, ATTN_KV_WIDTH), BF16),
        jax.ShapeDtypeStruct((bsz, seq, DN_CONV_DIM), BF16),
        jax.ShapeDtypeStruct((bsz, seq, LANES), F32),
        jax.ShapeDtypeStruct((bsz, seq, DN_WIDTH), BF16),
        jax.ShapeDtypeStruct((bsz, seq, 2 * D_MODEL), BF16),
    ]
    return pl.pallas_call(
        _inproj_kernel,
        grid=grid,
        in_specs=[tile(d),
                  pl.BlockSpec((1, 6, d), lambda b, t: (b, 0, 0)),
                  tile(1),
                  _const_spec((1, d)),
                  _const_spec(w_packed.shape),
                  _const_spec(bd.shape),
                  _const_spec(qkg.shape),
                  _const_spec(freq.shape),
                  _const_spec(convw.shape),
                  _const_spec(alog.shape),
                  _const_spec(dtb.shape)],
        out_specs=[tile(ATTN_Q_WIDTH), tile(ATTN_KV_WIDTH), tile(ATTN_KV_WIDTH), tile(DN_CONV_DIM),
                   tile(LANES), tile(DN_WIDTH), tile(2 * D_MODEL)],
        out_shape=out_shapes,
        scratch_shapes=[pltpu.VMEM((tt + 8, DN_CONV_DIM), F32)],
        compiler_params=_params(2),
        name="inproj",
    )(x, mod, pos, g1, w_packed, bd, qkg, freq, convw, alog, dtb)


def _attn_kernel(q_ref, k_ref, v_ref, kp_ref, vp_ref, sink_ref, o_ref, kw_ref, vw_ref):
    t = pl.program_id(1)
    tt = q_ref.shape[1]
    blk = ATTN_BLOCK
    hd = ATTN_HEAD_DIM

    kw_ref[blk:, :] = k_ref[0]
    vw_ref[blk:, :] = v_ref[0]

    @pl.when(t == 0)
    def _():
        kw_ref[:blk, :] = jnp.zeros((blk, ATTN_KV_WIDTH), BF16)
        vw_ref[:blk, :] = jnp.zeros((blk, ATTN_KV_WIDTH), BF16)

    @pl.when(t > 0)
    def _():
        kw_ref[:blk, :] = kp_ref[0]
        vw_ref[:blk, :] = vp_ref[0]

    row = lax.broadcasted_iota(jnp.int32, (blk, 2 * blk), 0)
    col = lax.broadcasted_iota(jnp.int32, (blk, 2 * blk), 1)
    band = (col > row) & (col <= row + blk)
    lo_kv = lax.broadcasted_iota(jnp.int32, (2 * blk, LANES), 1) < hd
    lo_q = lax.broadcasted_iota(jnp.int32, (blk, LANES), 1) < hd
    ones_lo = jnp.where(lo_kv, 1.0, 0.0).astype(BF16)
    ones_hi = jnp.where(lo_kv, 0.0, 1.0).astype(BF16)

    for j in range(tt // blk):
        valid = band & (col >= jnp.where(t == 0, blk, 0)) if j == 0 else band
        kwin = kw_ref[j * blk:(j + 2) * blk, :].astype(F32)
        vwin = vw_ref[j * blk:(j + 2) * blk, :].astype(F32)
        kswp = pltpu.roll(kwin, hd, axis=1)
        vswp = pltpu.roll(vwin, hd, axis=1)
        for g in range(ATTN_KV_HEADS):
            k_lo = jnp.where(lo_kv, kwin if g == 0 else kswp, 0.0).astype(BF16)
            k_hi = jnp.where(lo_kv, 0.0, kswp if g == 0 else kwin).astype(BF16)
            v_lo = jnp.where(lo_kv, vwin if g == 0 else vswp, 0.0).astype(BF16)
            v_hi = jnp.where(lo_kv, 0.0, vswp if g == 0 else vwin).astype(BF16)
            k_ext = jnp.concatenate([k_lo, k_hi], axis=0)
            v_ext = jnp.concatenate(
                [jnp.concatenate([v_lo, ones_lo], axis=1),
                 jnp.concatenate([v_hi, ones_hi], axis=1)], axis=0)
            for p in range(2 * g, 2 * g + 2):
                qp = q_ref[0, j * blk:(j + 1) * blk, p * LANES:(p + 1) * LANES]
                s = _dot_nt(qp, k_ext)
                sink_pair = sink_ref[p:p + 1, :]
                ps = []
                ms = []
                for e in range(2):
                    se = jnp.where(valid, s[:, e * 2 * blk:(e + 1) * 2 * blk], -jnp.inf)
                    m = jnp.maximum(jnp.max(se, axis=-1, keepdims=True),
                                    sink_ref[p:p + 1, e * hd:e * hd + 1])
                    ps.append(jnp.exp(se - m).astype(BF16))
                    ms.append(m)
                pv = _dot(jnp.concatenate(ps, axis=1), v_ext)
                m_pair = jnp.where(lo_q, ms[0], ms[1])
                denom = pv[:, LANES:] + jnp.exp(sink_pair - m_pair)
                o_ref[0, j * blk:(j + 1) * blk, p * LANES:(p + 1) * LANES] = (
                    pv[:, :LANES] / denom).astype(BF16)


def _attn(q, k, v, sink_lanes, tt):
    bsz, seq, _ = q.shape
    nprev = tt // ATTN_BLOCK

    def tile(width):
        return pl.BlockSpec((1, tt, width), lambda b, t: (b, t, 0))

    def prev(width):
        return pl.BlockSpec((1, ATTN_BLOCK, width),
                            lambda b, t: (b, jnp.maximum(t * nprev - 1, 0), 0))

    return pl.pallas_call(
        _attn_kernel,
        grid=(bsz, seq // tt),
        in_specs=[tile(ATTN_Q_WIDTH), tile(ATTN_KV_WIDTH), tile(ATTN_KV_WIDTH),
                  prev(ATTN_KV_WIDTH), prev(ATTN_KV_WIDTH),
                  _const_spec(sink_lanes.shape)],
        out_specs=tile(ATTN_Q_WIDTH),
        out_shape=jax.ShapeDtypeStruct((bsz, seq, ATTN_Q_WIDTH), BF16),
        scratch_shapes=[pltpu.VMEM((tt + ATTN_BLOCK, ATTN_KV_WIDTH), BF16),
                        pltpu.VMEM((tt + ATTN_BLOCK, ATTN_KV_WIDTH), BF16)],
        compiler_params=_params(2),
        name="attn",
    )(q, k, v, k, v, sink_lanes)


def _gdn_kernel(dqkv_ref, bg_ref, zs_ref, gain_ref, bcat_ref, o_ref, state_ref, gc_ref):
    t = pl.program_id(1)
    tt = dqkv_ref.shape[1]
    c = DN_CHUNK
    n_chunks = tt // c

    @pl.when(t == 0)
    def _():
        state_ref[...] = jnp.zeros(state_ref.shape, F32)

    bg = bg_ref[0]
    rin = lax.broadcasted_iota(jnp.int32, bg.shape, 0) % c
    gc = bg
    step = 1
    while step < c:
        gc = gc + jnp.where(rin >= step, pltpu.roll(gc, step, axis=0), 0.0)
        step *= 2
    gc_ref[...] = gc
    gct = gc.T

    ri = lax.broadcasted_iota(jnp.int32, (c, c), 0)
    ci = lax.broadcasted_iota(jnp.int32, (c, c), 1)
    incl = ri >= ci
    strict = ri > ci
    sb = DN_SUB
    diag_blocks = (ri // sb) == (ci // sb)
    merge_masks = []
    width = sb
    while width < c:
        merge_masks.append(((ri // (2 * width)) == (ci // (2 * width))) & ((ri // width) != (ci // width)))
        width *= 2
    lane_blk = lax.broadcasted_iota(jnp.int32, (sb, c), 1) // sb
    sub_p = lax.broadcasted_iota(jnp.int32, (sb, LANES), 0)
    lane_p = lax.broadcasted_iota(jnp.int32, (sb, LANES), 1)
    eye_packed = jnp.where(sub_p == lane_p % sb, 1.0, 0.0).astype(F32)

    work = {}
    packed = []
    for ch in range(n_chunks):
        r0 = ch * c
        for h in range(DN_HEADS):
            q = dqkv_ref[0, r0:r0 + c, h * DN_DIM:(h + 1) * DN_DIM].astype(F32)
            k = dqkv_ref[0, r0:r0 + c, DN_WIDTH + h * DN_DIM:DN_WIDTH + (h + 1) * DN_DIM].astype(F32)
            v = dqkv_ref[0, r0:r0 + c, 2 * DN_WIDTH + h * DN_DIM:2 * DN_WIDTH + (h + 1) * DN_DIM].astype(F32)
            beta = bg_ref[0, r0:r0 + c, h:h + 1]
            gcc = gc_ref[r0:r0 + c, DN_HEADS + h:DN_HEADS + h + 1]
            gcr = gct[DN_HEADS + h:DN_HEADS + h + 1, r0:r0 + c]
            g_last = gcc[c - 1:c, :]
            eg = jnp.exp(gcc)
            kb = k * beta
            decay = jnp.exp(jnp.where(incl, gcc - gcr, -jnp.inf))
            a = _dot_nt(jnp.concatenate([kb, q], axis=0).astype(BF16), k.astype(BF16))
            l_mat = jnp.where(strict, a[:c] * decay, 0.0)
            d8 = jnp.zeros((sb, c), F32)
            for g in range(c // sb):
                d8 = d8 + jnp.where(lane_blk == g, l_mat[g * sb:(g + 1) * sb, :], 0.0)
            packed.append(d8)
            work[ch, h] = dict(
                l=l_mat, a_intra=(a[c:] * decay).astype(BF16),
                rhs=jnp.concatenate([v * beta, kb * eg], axis=1).astype(BF16),
                q_dec=q * eg, k_dec_t=(k * jnp.exp(g_last - gcc)).T.astype(BF16),
                d_chunk=jnp.exp(g_last))

    pall = jnp.concatenate(
        [jnp.concatenate(packed[2 * i:2 * i + 2], axis=1) for i in range(len(packed) // 2)], axis=0)
    p_hi = pall.astype(BF16)
    p_lo = (pall - p_hi.astype(F32)).astype(BF16)
    col_bcast = _dot(p_hi, bcat_ref[...]) + _dot(p_lo, bcat_ref[...])
    idx = 0
    for ch in range(n_chunks):
        for h in range(DN_HEADS):
            pair, e = divmod(idx, 2)
            idx += 1
            if e == 0:
                tp = eye_packed
                for j in range(sb - 1):
                    tp = tp - col_bcast[pair * sb:(pair + 1) * sb, j * LANES:(j + 1) * LANES] * tp[j:j + 1, :]
            wk = work[ch, h]
            tinv = jnp.where(diag_blocks, jnp.tile(tp[:, e * c:(e + 1) * c], (c // sb, 1)), 0.0)
            for mask in merge_masks:
                tb = tinv.astype(BF16)
                lower = jnp.where(mask, wk["l"], 0.0).astype(BF16)
                tinv = tinv - _dot(_dot(tb, lower).astype(BF16), tb)
            wk["uw"] = _dot(tinv.astype(BF16), wk["rhs"])

    for ch in range(n_chunks):
        r0 = ch * c
        for h in range(DN_HEADS):
            wk = work[ch, h]
            u = wk["uw"][:, :DN_DIM]
            w = wk["uw"][:, DN_DIM:]
            state = state_ref[h]
            ws_qs = _dot(jnp.concatenate([w, wk["q_dec"]], axis=0).astype(BF16), state.astype(BF16))
            v_new = (u - ws_qs[:c]).astype(BF16)
            o = ws_qs[c:] + _dot(wk["a_intra"], v_new)
            state_ref[h] = state * wk["d_chunk"] + _dot(wk["k_dec_t"], v_new)
            on = o * lax.rsqrt(jnp.mean(o * o, axis=-1, keepdims=True) + NORM_EPS) * gain_ref[...]
            zs = zs_ref[0, r0:r0 + c, h * DN_DIM:(h + 1) * DN_DIM].astype(F32)
            o_ref[0, r0:r0 + c, h * DN_DIM:(h + 1) * DN_DIM] = (on * zs).astype(BF16)


def _gdn_bcast_matrix():
    sb = DN_SUB
    mat = np.zeros((LANES, (sb - 1) * LANES), np.float32)
    for j in range(sb - 1):
        for m in range(LANES // sb):
            mat[sb * m + j, j * LANES + sb * m:j * LANES + sb * (m + 1)] = 1.0
    return mat


def _gdn(dqkv, bg, zs, gain, tt):
    bsz, seq, _ = dqkv.shape

    def tile(width):
        return pl.BlockSpec((1, tt, width), lambda b, t: (b, t, 0))

    bcat = jnp.asarray(_gdn_bcast_matrix(), BF16)
    return pl.pallas_call(
        _gdn_kernel,
        grid=(bsz, seq // tt),
        in_specs=[tile(DN_CONV_DIM), tile(LANES), tile(DN_WIDTH), _const_spec(gain.shape),
                  _const_spec(bcat.shape)],
        out_specs=tile(DN_WIDTH),
        out_shape=jax.ShapeDtypeStruct((bsz, seq, DN_WIDTH), BF16),
        scratch_shapes=[pltpu.VMEM((DN_HEADS, DN_DIM, DN_DIM), F32),
                        pltpu.VMEM((tt, LANES), F32)],
        compiler_params=_params(2),
        name="gdn",
    )(dqkv, bg, zs, gain, bcat)


def _merge_kernel(x_ref, oa_ref, od_ref, sg_ref, mod_ref, wb_ref, wo_ref, o_ref):
    ya = _dot(oa_ref[0], wb_ref[:ATTN_Q_WIDTH, :])
    yd = _dot(od_ref[0], wb_ref[ATTN_Q_WIDTH:, :])
    merged = sg_ref[0, :, :D_MODEL].astype(F32) * ya + sg_ref[0, :, D_MODEL:].astype(F32) * yd
    out = _dot(merged.astype(BF16), wo_ref[...])
    o_ref[0] = x_ref[0] + mod_ref[0, 2:3, :] * out


def _merge(x, oa, od, sg, mod, wb, wo, tt):
    bsz, seq, d = x.shape

    def tile(width):
        return pl.BlockSpec((1, tt, width), lambda b, t: (b, t, 0))

    return pl.pallas_call(
        _merge_kernel,
        grid=(bsz, seq // tt),
        in_specs=[tile(d), tile(ATTN_Q_WIDTH), tile(DN_WIDTH), tile(2 * d),
                  pl.BlockSpec((1, 6, d), lambda b, t: (b, 0, 0)),
                  _const_spec(wb.shape), _const_spec(wo.shape)],
        out_specs=tile(d),
        out_shape=jax.ShapeDtypeStruct((bsz, seq, d), F32),
        compiler_params=_params(2),
        name="merge",
    )(x, oa, od, sg, mod, wb, wo)


def _ffn_kernel(x_ref, mod_ref, g2_ref, wgu_ref, wd_ref, o_ref):
    x = x_ref[0]
    ms = jnp.mean(x * x, axis=-1, keepdims=True)
    y = x * lax.rsqrt(ms + NORM_EPS) * g2_ref[...]
    h = (y * (1.0 + mod_ref[0, 4:5, :]) + mod_ref[0, 3:4, :]).astype(BF16)
    gate = _dot(h, wgu_ref[:, :FFN_HIDDEN])
    up = _dot(h, wgu_ref[:, FFN_HIDDEN:])
    act = (gate * _sigmoid(gate) * up).astype(BF16)
    o_ref[0] = x + mod_ref[0, 5:6, :] * _dot(act, wd_ref[...])


def _ffn(x, mod, g2, wgu, wd, tt):
    bsz, seq, d = x.shape

    def tile(width):
        return pl.BlockSpec((1, tt, width), lambda b, t: (b, t, 0))

    return pl.pallas_call(
        _ffn_kernel,
        grid=(bsz, seq // tt),
        in_specs=[tile(d), pl.BlockSpec((1, 6, d), lambda b, t: (b, 0, 0)),
                  _const_spec((1, d)), _const_spec(wgu.shape), _const_spec(wd.shape)],
        out_specs=tile(d),
        out_shape=jax.ShapeDtypeStruct((bsz, seq, d), F32),
        compiler_params=_params(2),
        name="ffn",
    )(x, mod, g2, wgu, wd)


def _rope_tables():
    half = ROT_DIM // 2
    inv_freq = ROPE_THETA ** (-np.arange(0, ROT_DIM, 2, dtype=np.float32) / ROT_DIM)
    tab = np.zeros((8, LANES), np.float32)
    for j in range(LANES):
        r = j % ATTN_HEAD_DIM
        if r < ROT_DIM:
            tab[0, j] = inv_freq[r % half]
            tab[1, j] = -1.0 if r < half else 0.0
            tab[2, j] = 1.0 if r >= half else 0.0
    return tab


def _pick_tile(seq, want):
    tt = min(want, seq)
    assert seq % tt == 0 and tt % ATTN_BLOCK == 0, (seq, tt)
    return tt


def kernel(x, c, positions, ada_w, ada_b, norm1_g, w_in, conv_w, q_norm_g, k_norm_g, sinks, a_log,
           dt_bias, dn_norm_g, w_branch, w_out, norm2_g, w_gate_up, w_down):
    bsz, seq, d = x.shape
    assert d == D_MODEL and ada_w.shape[0] == 1, "single-layer kernel"
    n_in = ATTN_Q_WIDTH + 2 * ATTN_KV_WIDTH + DN_CONV_DIM

    mod = _ada(c, ada_w[0], ada_b[0]).reshape(bsz, 6, d)

    w = w_in[0]
    w_packed = jnp.concatenate(
        [w[:, :n_in], w[:, n_in:n_in + 2 * DN_HEADS],
         jnp.zeros((d, LANES - 2 * DN_HEADS), w.dtype), w[:, n_in + 2 * DN_HEADS:]],
        axis=1).astype(BF16)
    assert w_packed.shape[1] == IN_PACKED

    head_id = np.arange(ATTN_Q_WIDTH) // ATTN_HEAD_DIM
    bd = jnp.asarray(head_id[:, None] == head_id[None, :], BF16)
    qkg = jnp.stack([jnp.tile(q_norm_g[0], ATTN_HEADS),
                     jnp.pad(jnp.tile(k_norm_g[0], ATTN_KV_HEADS), (0, ATTN_Q_WIDTH - ATTN_KV_WIDTH))])
    freq = jnp.asarray(_rope_tables())
    convw = conv_w[0].reshape(DN_CONV, DN_CONV_DIM)
    pad4 = (DN_HEADS, LANES - 2 * DN_HEADS)
    alog = jnp.pad(a_log[0], pad4).reshape(1, LANES)
    dtb = jnp.pad(dt_bias[0], pad4).reshape(1, LANES)
    sink_lanes = jnp.repeat(sinks[0], ATTN_HEAD_DIM).reshape(ATTN_HEADS // 2, LANES)

    q, k, v, dqkv, bg, zs, sg = _inproj(
        x, mod, positions.reshape(bsz, seq, 1), norm1_g[0].reshape(1, d), w_packed, bd, qkg, freq,
        convw, alog, dtb, _pick_tile(seq, 256))
    oa = _attn(q, k, v, sink_lanes, _pick_tile(seq, 512))
    od = _gdn(dqkv, bg, zs, dn_norm_g[0].reshape(1, DN_DIM), _pick_tile(seq, 256))
    x1 = _merge(x, oa, od, sg, mod, w_branch[0].astype(BF16), w_out[0].astype(BF16),
                _pick_tile(seq, 512))
    return _ffn(x1, mod, norm2_g[0].reshape(1, d), w_gate_up[0].astype(BF16),
                w_down[0].astype(BF16), _pick_tile(seq, 256))
```

```python
import functools

import jax
import jax.numpy as jnp
import numpy as np
from jax import lax
from jax.experimental import pallas as pl
from jax.experimental.pallas import tpu as pltpu

F32 = jnp.float32
BF16 = jnp.bfloat16

D_MODEL = 1024
ATTN_HEADS = 8
ATTN_KV_HEADS = 2
ATTN_HEAD_DIM = 64
ATTN_BLOCK = 128
ROT_DIM = ATTN_HEAD_DIM // 4
ROPE_THETA = 500000.0
ATTN_Q_WIDTH = ATTN_HEADS * ATTN_HEAD_DIM
ATTN_KV_WIDTH = ATTN_KV_HEADS * ATTN_HEAD_DIM
DN_HEADS = 4
DN_DIM = 128
DN_CONV = 4
DN_CHUNK = 64
DN_SUB = 8
DN_WIDTH = DN_HEADS * DN_DIM
DN_CONV_DIM = 3 * DN_WIDTH
FFN_HIDDEN = 2816
NORM_EPS = 1e-6
LANES = 128

COL_QKV = 0
COL_DN = COL_QKV + ATTN_Q_WIDTH + 2 * ATTN_KV_WIDTH
COL_BA = COL_DN + DN_CONV_DIM
COL_Z = COL_BA + LANES
COL_GATES = COL_Z + DN_WIDTH
IN_PACKED = COL_GATES + 2 * D_MODEL

VMEM_LIMIT = 56 * 1024 * 1024


def _sigmoid(x):
    return 1.0 / (1.0 + jnp.exp(-x))


def _dot(a, b):
    return jnp.dot(a, b, preferred_element_type=F32)


def _dot_nt(a, b):
    return lax.dot_general(a, b, (((1,), (1,)), ((), ())), preferred_element_type=F32)


def _const_spec(shape):
    nd = len(shape)
    return pl.BlockSpec(shape, lambda *_: (0,) * nd, pipeline_mode=pl.Buffered(1))


def _params(n_grid):
    return pltpu.CompilerParams(dimension_semantics=("arbitrary",) * n_grid,
                                vmem_limit_bytes=VMEM_LIMIT)


def _ada_kernel(c_ref, w_ref, b_ref, o_ref):
    c = c_ref[...]
    cond = c * _sigmoid(c)
    o_ref[...] = jnp.dot(cond, w_ref[...], preferred_element_type=F32,
                         precision=lax.Precision.HIGHEST) + b_ref[...]


def _ada(c, w, b):
    bsz, d = c.shape
    n = w.shape[1]
    bn = 1536
    return pl.pallas_call(
        _ada_kernel,
        grid=(n // bn,),
        in_specs=[pl.BlockSpec((bsz, d), lambda j: (0, 0)),
                  pl.BlockSpec((d, bn), lambda j: (0, j)),
                  pl.BlockSpec((1, bn), lambda j: (0, j))],
        out_specs=pl.BlockSpec((bsz, bn), lambda j: (0, j)),
        out_shape=jax.ShapeDtypeStruct((bsz, n), F32),
        compiler_params=_params(1),
        name="ada",
    )(c, w, b.reshape(1, n))


def _inproj_kernel(x_ref, mod_ref, pos_ref, g1_ref, w_ref, bd_ref, qkg_ref, freq_ref, convw_ref,
                   alog_ref, dtb_ref,
                   q_ref, k_ref, v_ref, dqkv_ref, bg_ref, zs_ref, sg_ref,
                   xc_ref):
    t = pl.program_id(1)
    tt = x_ref.shape[1]

    x = x_ref[0]
    ms = jnp.mean(x * x, axis=-1, keepdims=True)
    y = x * lax.rsqrt(ms + NORM_EPS) * g1_ref[...]
    shift1 = mod_ref[0, 0:1, :]
    scale1 = mod_ref[0, 1:2, :]
    h = (y * (1.0 + scale1) + shift1).astype(BF16)

    qkv = _dot(h, w_ref[:, COL_QKV:COL_DN])
    q = qkv[:, :ATTN_Q_WIDTH]
    k = qkv[:, ATTN_Q_WIDTH:ATTN_Q_WIDTH + ATTN_KV_WIDTH]
    v = qkv[:, ATTN_Q_WIDTH + ATTN_KV_WIDTH:]
    bd = bd_ref[...]
    ssq_q = _dot((q * q).astype(BF16), bd)
    ssq_k = _dot((k * k).astype(BF16), bd[:ATTN_KV_WIDTH, :ATTN_KV_WIDTH])
    qn = q * lax.rsqrt(ssq_q * (1.0 / ATTN_HEAD_DIM) + NORM_EPS) * qkg_ref[0:1, :]
    kn = k * lax.rsqrt(ssq_k * (1.0 / ATTN_HEAD_DIM) + NORM_EPS) * qkg_ref[1:2, :ATTN_KV_WIDTH]

    ang = pos_ref[0].astype(F32) * freq_ref[0:1, :]
    cos = jnp.cos(ang)
    sin = jnp.sin(ang)
    s_lo = sin * freq_ref[1:2, :]
    s_hi = sin * freq_ref[2:3, :]

    def rope(u):
        half = ROT_DIM // 2
        return (u * cos + pltpu.roll(u, LANES - half, axis=1) * s_lo
                + pltpu.roll(u, half, axis=1) * s_hi)

    for i in range(ATTN_Q_WIDTH // LANES):
        sl = slice(i * LANES, (i + 1) * LANES)
        q_ref[0, :, sl] = (rope(qn[:, sl]) * (ATTN_HEAD_DIM ** -0.5)).astype(BF16)
    k_ref[0] = rope(kn).astype(BF16)
    v_ref[0] = v.astype(BF16)

    @pl.when(t == 0)
    def _():
        xc_ref[0:8, :] = jnp.zeros((8, DN_CONV_DIM), F32)

    xc_ref[8:tt + 8, :] = _dot(h, w_ref[:, COL_DN:COL_BA])
    conv = convw_ref[3:4, :] * xc_ref[8:tt + 8, :]
    for i in range(DN_CONV - 1):
        off = 8 - (DN_CONV - 1) + i
        conv = conv + convw_ref[i:i + 1, :] * xc_ref[off:off + tt, :]
    xc_ref[0:8, :] = xc_ref[tt:tt + 8, :]
    act = conv * _sigmoid(conv)
    for i in range(2 * DN_HEADS):
        sl = slice(i * DN_DIM, (i + 1) * DN_DIM)
        u = act[:, sl]
        scale = DN_DIM ** -0.5 if i < DN_HEADS else 1.0
        un = u * (lax.rsqrt(jnp.sum(u * u, axis=-1, keepdims=True) + NORM_EPS) * scale)
        dqkv_ref[0, :, sl] = un.astype(BF16)
    dqkv_ref[0, :, 2 * DN_WIDTH:] = act[:, 2 * DN_WIDTH:].astype(BF16)

    ba = _dot(h, w_ref[:, COL_BA:COL_Z])
    sp_in = ba + dtb_ref[...]
    softplus = jnp.maximum(sp_in, 0.0) + jnp.log(1.0 + jnp.exp(-jnp.abs(sp_in)))
    g = -jnp.exp(alog_ref[...]) * softplus
    lane = lax.broadcasted_iota(jnp.int32, ba.shape, 1)
    bg_ref[0] = jnp.where(lane < DN_HEADS, _sigmoid(ba), g)

    z = _dot(h, w_ref[:, COL_Z:COL_GATES])
    zs_ref[0] = (z * _sigmoid(z)).astype(BF16)
    for i in range(2):
        sl = slice(COL_GATES + i * D_MODEL, COL_GATES + (i + 1) * D_MODEL)
        sg_ref[0, :, i * D_MODEL:(i + 1) * D_MODEL] = _sigmoid(_dot(h, w_ref[:, sl])).astype(BF16)


def _inproj(x, mod, pos, g1, w_packed, bd, qkg, freq, convw, alog, dtb, tt):
    bsz, seq, d = x.shape
    grid = (bsz, seq // tt)

    def tile(width):
        return pl.BlockSpec((1, tt, width), lambda b, t: (b, t, 0))

    out_shapes = [
        jax.ShapeDtypeStruct((bsz, seq, ATTN_Q_WIDTH), BF16),
        jax.ShapeDtypeStruct((bsz, seq, ATTN_KV_WIDTH), BF16),
        jax.ShapeDtypeStruct((bsz, seq, ATTN_KV_WIDTH), BF16),
        jax.ShapeDtypeStruct((bsz, seq, DN_CONV_DIM), BF16),
        jax.ShapeDtypeStruct((bsz, seq, LANES), F32),
        jax.ShapeDtypeStruct((bsz, seq, DN_WIDTH), BF16),
        jax.ShapeDtypeStruct((bsz, seq, 2 * D_MODEL), BF16),
    ]
    return pl.pallas_call(
        _inproj_kernel,
        grid=grid,
        in_specs=[tile(d),
                  pl.BlockSpec((1, 6, d), lambda b, t: (b, 0, 0)),
                  tile(1),
                  _const_spec((1, d)),
                  _const_spec(w_packed.shape),
                  _const_spec(bd.shape),
                  _const_spec(qkg.shape),
                  _const_spec(freq.shape),
                  _const_spec(convw.shape),
                  _const_spec(alog.shape),
                  _const_spec(dtb.shape)],
        out_specs=[tile(ATTN_Q_WIDTH), tile(ATTN_KV_WIDTH), tile(ATTN_KV_WIDTH), tile(DN_CONV_DIM),
                   tile(LANES), tile(DN_WIDTH), tile(2 * D_MODEL)],
        out_shape=out_shapes,
        scratch_shapes=[pltpu.VMEM((tt + 8, DN_CONV_DIM), F32)],
        compiler_params=_params(2),
        name="inproj",
    )(x, mod, pos, g1, w_packed, bd, qkg, freq, convw, alog, dtb)


def _attn_kernel(q_ref, k_ref, v_ref, kp_ref, vp_ref, sink_ref, o_ref, kw_ref, vw_ref):
    t = pl.program_id(1)
    tt = q_ref.shape[1]
    blk = ATTN_BLOCK
    hd = ATTN_HEAD_DIM

    kw_ref[blk:, :] = k_ref[0]
    vw_ref[blk:, :] = v_ref[0]

    @pl.when(t == 0)
    def _():
        kw_ref[:blk, :] = jnp.zeros((blk, ATTN_KV_WIDTH), BF16)
        vw_ref[:blk, :] = jnp.zeros((blk, ATTN_KV_WIDTH), BF16)

    @pl.when(t > 0)
    def _():
        kw_ref[:blk, :] = kp_ref[0]
        vw_ref[:blk, :] = vp_ref[0]

    row = lax.broadcasted_iota(jnp.int32, (blk, 2 * blk), 0)
    col = lax.broadcasted_iota(jnp.int32, (blk, 2 * blk), 1)
    band = (col > row) & (col <= row + blk)
    lo_kv = lax.broadcasted_iota(jnp.int32, (2 * blk, LANES), 1) < hd
    lo_q = lax.broadcasted_iota(jnp.int32, (blk, LANES), 1) < hd
    ones_lo = jnp.where(lo_kv, 1.0, 0.0).astype(BF16)
    ones_hi = jnp.where(lo_kv, 0.0, 1.0).astype(BF16)

    for j in range(tt // blk):
        valid = band & (col >= jnp.where(t == 0, blk, 0)) if j == 0 else band
        kwin = kw_ref[j * blk:(j + 2) * blk, :].astype(F32)
        vwin = vw_ref[j * blk:(j + 2) * blk, :].astype(F32)
        kswp = pltpu.roll(kwin, hd, axis=1)
        vswp = pltpu.roll(vwin, hd, axis=1)
        for g in range(ATTN_KV_HEADS):
            k_lo = jnp.where(lo_kv, kwin if g == 0 else kswp, 0.0).astype(BF16)
            k_hi = jnp.where(lo_kv, 0.0, kswp if g == 0 else kwin).astype(BF16)
            v_lo = jnp.where(lo_kv, vwin if g == 0 else vswp, 0.0).astype(BF16)
            v_hi = jnp.where(lo_kv, 0.0, vswp if g == 0 else vwin).astype(BF16)
            k_ext = jnp.concatenate([k_lo, k_hi], axis=0)
            v_ext = jnp.concatenate(
                [jnp.concatenate([v_lo, ones_lo], axis=1),
                 jnp.concatenate([v_hi, ones_hi], axis=1)], axis=0)
            for p in range(2 * g, 2 * g + 2):
                qp = q_ref[0, j * blk:(j + 1) * blk, p * LANES:(p + 1) * LANES]
                s = _dot_nt(qp, k_ext)
                sink_pair = sink_ref[p:p + 1, :]
                ps = []
                ms = []
                for e in range(2):
                    se = jnp.where(valid, s[:, e * 2 * blk:(e + 1) * 2 * blk], -jnp.inf)
                    m = jnp.maximum(jnp.max(se, axis=-1, keepdims=True),
                                    sink_ref[p:p + 1, e * hd:e * hd + 1])
                    ps.append(jnp.exp(se - m).astype(BF16))
                    ms.append(m)
                pv = _dot(jnp.concatenate(ps, axis=1), v_ext)
                m_pair = jnp.where(lo_q, ms[0], ms[1])
                denom = pv[:, LANES:] + jnp.exp(sink_pair - m_pair)
                o_ref[0, j * blk:(j + 1) * blk, p * LANES:(p + 1) * LANES] = (
                    pv[:, :LANES] / denom).astype(BF16)


def _attn(q, k, v, sink_lanes, tt):
    bsz, seq, _ = q.shape
    nprev = tt // ATTN_BLOCK

    def tile(width):
        return pl.BlockSpec((1, tt, width), lambda b, t: (b, t, 0))

    def prev(width):
        return pl.BlockSpec((1, ATTN_BLOCK, width),
                            lambda b, t: (b, jnp.maximum(t * nprev - 1, 0), 0))

    return pl.pallas_call(
        _attn_kernel,
        grid=(bsz, seq // tt),
        in_specs=[tile(ATTN_Q_WIDTH), tile(ATTN_KV_WIDTH), tile(ATTN_KV_WIDTH),
                  prev(ATTN_KV_WIDTH), prev(ATTN_KV_WIDTH),
                  _const_spec(sink_lanes.shape)],
        out_specs=tile(ATTN_Q_WIDTH),
        out_shape=jax.ShapeDtypeStruct((bsz, seq, ATTN_Q_WIDTH), BF16),
        scratch_shapes=[pltpu.VMEM((tt + ATTN_BLOCK, ATTN_KV_WIDTH), BF16),
                        pltpu.VMEM((tt + ATTN_BLOCK, ATTN_KV_WIDTH), BF16)],
        compiler_params=_params(2),
        name="attn",
    )(q, k, v, k, v, sink_lanes)


def _gdn_kernel(dqkv_ref, bg_ref, zs_ref, gain_ref, bcat_ref, o_ref, state_ref, gc_ref):
    t = pl.program_id(1)
    tt = dqkv_ref.shape[1]
    c = DN_CHUNK
    n_chunks = tt // c

    @pl.when(t == 0)
    def _():
        state_ref[...] = jnp.zeros(state_ref.shape, F32)

    bg = bg_ref[0]
    rin = lax.broadcasted_iota(jnp.int32, bg.shape, 0) % c
    gc = bg
    step = 1
    while step < c:
        gc = gc + jnp.where(rin >= step, pltpu.roll(gc, step, axis=0), 0.0)
        step *= 2
    gc_ref[...] = gc
    gct = gc.T

    ri = lax.broadcasted_iota(jnp.int32, (c, c), 0)
    ci = lax.broadcasted_iota(jnp.int32, (c, c), 1)
    incl = ri >= ci
    strict = ri > ci
    sb = DN_SUB
    diag_blocks = (ri // sb) == (ci // sb)
    merge_masks = []
    width = sb
    while width < c:
        merge_masks.append(((ri // (2 * width)) == (ci // (2 * width))) & ((ri // width) != (ci // width)))
        width *= 2
    lane_blk = lax.broadcasted_iota(jnp.int32, (sb, c), 1) // sb
    sub_p = lax.broadcasted_iota(jnp.int32, (sb, LANES), 0)
    lane_p = lax.broadcasted_iota(jnp.int32, (sb, LANES), 1)
    eye_packed = jnp.where(sub_p == lane_p % sb, 1.0, 0.0).astype(F32)

    work = {}
    packed = []
    for ch in range(n_chunks):
        r0 = ch * c
        for h in range(DN_HEADS):
            q = dqkv_ref[0, r0:r0 + c, h * DN_DIM:(h + 1) * DN_DIM].astype(F32)
            k = dqkv_ref[0, r0:r0 + c, DN_WIDTH + h * DN_DIM:DN_WIDTH + (h + 1) * DN_DIM].astype(F32)
            v = dqkv_ref[0, r0:r0 + c, 2 * DN_WIDTH + h * DN_DIM:2 * DN_WIDTH + (h + 1) * DN_DIM].astype(F32)
            beta = bg_ref[0, r0:r0 + c, h:h + 1]
            gcc = gc_ref[r0:r0 + c, DN_HEADS + h:DN_HEADS + h + 1]
            gcr = gct[DN_HEADS + h:DN_HEADS + h + 1, r0:r0 + c]
            g_last = gcc[c - 1:c, :]
            eg = jnp.exp(gcc)
            kb = k * beta
            decay = jnp.exp(jnp.where(incl, gcc - gcr, -jnp.inf))
            a = _dot_nt(jnp.concatenate([kb, q], axis=0).astype(BF16), k.astype(BF16))
            l_mat = jnp.where(strict, a[:c] * decay, 0.0)
            d8 = jnp.zeros((sb, c), F32)
            for g in range(c // sb):
                d8 = d8 + jnp.where(lane_blk == g, l_mat[g * sb:(g + 1) * sb, :], 0.0)
            packed.append(d8)
            work[ch, h] = dict(
                l=l_mat, a_intra=(a[c:] * decay).astype(BF16),
                rhs=jnp.concatenate([v * beta, kb * eg], axis=1).astype(BF16),
                q_dec=q * eg, k_dec_t=(k * jnp.exp(g_last - gcc)).T.astype(BF16),
                d_chunk=jnp.exp(g_last))

    pall = jnp.concatenate(
        [jnp.concatenate(packed[2 * i:2 * i + 2], axis=1) for i in range(len(packed) // 2)], axis=0)
    p_hi = pall.astype(BF16)
    p_lo = (pall - p_hi.astype(F32)).astype(BF16)
    col_bcast = _dot(p_hi, bcat_ref[...]) + _dot(p_lo, bcat_ref[...])
    keys = [(ch, h) for ch in range(n_chunks) for h in range(DN_HEADS)]
    n_pairs = len(keys) // 2
    tps = [eye_packed] * n_pairs
    for j in range(sb - 1):
        tps = [tp - col_bcast[p * sb:(p + 1) * sb, j * LANES:(j + 1) * LANES] * tp[j:j + 1, :]
               for p, tp in enumerate(tps)]
    tinv = {key: jnp.where(diag_blocks, jnp.tile(tps[i // 2][:, (i % 2) * c:(i % 2 + 1) * c], (c // sb, 1)), 0.0)
            for i, key in enumerate(keys)}
    for mask in merge_masks:
        tb = {key: tinv[key].astype(BF16) for key in keys}
        half = {key: _dot(tb[key], jnp.where(mask, work[key]["l"], 0.0).astype(BF16)).astype(BF16)
                for key in keys}
        tinv = {key: tinv[key] - _dot(half[key], tb[key]) for key in keys}
    uw = {key: _dot(tinv[key].astype(BF16), work[key]["rhs"]).astype(BF16) for key in keys}
    a_uw = {key: _dot(work[key]["a_intra"], uw[key]) for key in keys}
    k_uw = {key: _dot(work[key]["k_dec_t"], uw[key]) for key in keys}

    state = [state_ref[h] for h in range(DN_HEADS)]
    for ch in range(n_chunks):
        r0 = ch * c
        lhs = {h: jnp.concatenate([work[ch, h]["q_dec"] - a_uw[ch, h][:, DN_DIM:], k_uw[ch, h][:, DN_DIM:]],
                                  axis=0).astype(BF16) for h in range(DN_HEADS)}
        prod = {h: _dot(lhs[h], state[h].astype(BF16)) for h in range(DN_HEADS)}
        for h in range(DN_HEADS):
            o = prod[h][:c] + a_uw[ch, h][:, :DN_DIM]
            state[h] = state[h] * work[ch, h]["d_chunk"] - prod[h][c:] + k_uw[ch, h][:, :DN_DIM]
            on = o * lax.rsqrt(jnp.mean(o * o, axis=-1, keepdims=True) + NORM_EPS) * gain_ref[...]
            zs = zs_ref[0, r0:r0 + c, h * DN_DIM:(h + 1) * DN_DIM].astype(F32)
            o_ref[0, r0:r0 + c, h * DN_DIM:(h + 1) * DN_DIM] = (on * zs).astype(BF16)
    for h in range(DN_HEADS):
        state_ref[h] = state[h]


def _gdn_bcast_matrix():
    sb = DN_SUB
    mat = np.zeros((LANES, (sb - 1) * LANES), np.float32)
    for j in range(sb - 1):
        for m in range(LANES // sb):
            mat[sb * m + j, j * LANES + sb * m:j * LANES + sb * (m + 1)] = 1.0
    return mat


def _gdn(dqkv, bg, zs, gain, tt):
    bsz, seq, _ = dqkv.shape

    def tile(width):
        return pl.BlockSpec((1, tt, width), lambda b, t: (b, t, 0))

    bcat = jnp.asarray(_gdn_bcast_matrix(), BF16)
    return pl.pallas_call(
        _gdn_kernel,
        grid=(bsz, seq // tt),
        in_specs=[tile(DN_CONV_DIM), tile(LANES), tile(DN_WIDTH), _const_spec(gain.shape),
                  _const_spec(bcat.shape)],
        out_specs=tile(DN_WIDTH),
        out_shape=jax.ShapeDtypeStruct((bsz, seq, DN_WIDTH), BF16),
        scratch_shapes=[pltpu.VMEM((DN_HEADS, DN_DIM, DN_DIM), F32),
                        pltpu.VMEM((tt, LANES), F32)],
        compiler_params=_params(2),
        name="gdn",
    )(dqkv, bg, zs, gain, bcat)


def _merge_kernel(x_ref, oa_ref, od_ref, sg_ref, mod_ref, wb_ref, wo_ref, o_ref):
    ya = _dot(oa_ref[0], wb_ref[:ATTN_Q_WIDTH, :])
    yd = _dot(od_ref[0], wb_ref[ATTN_Q_WIDTH:, :])
    merged = sg_ref[0, :, :D_MODEL].astype(F32) * ya + sg_ref[0, :, D_MODEL:].astype(F32) * yd
    out = _dot(merged.astype(BF16), wo_ref[...])
    o_ref[0] = x_ref[0] + mod_ref[0, 2:3, :] * out


def _merge(x, oa, od, sg, mod, wb, wo, tt):
    bsz, seq, d = x.shape

    def tile(width):
        return pl.BlockSpec((1, tt, width), lambda b, t: (b, t, 0))

    return pl.pallas_call(
        _merge_kernel,
        grid=(bsz, seq // tt),
        in_specs=[tile(d), tile(ATTN_Q_WIDTH), tile(DN_WIDTH), tile(2 * d),
                  pl.BlockSpec((1, 6, d), lambda b, t: (b, 0, 0)),
                  _const_spec(wb.shape), _const_spec(wo.shape)],
        out_specs=tile(d),
        out_shape=jax.ShapeDtypeStruct((bsz, seq, d), F32),
        compiler_params=_params(2),
        name="merge",
    )(x, oa, od, sg, mod, wb, wo)


def _ffn_kernel(x_ref, mod_ref, g2_ref, wgu_ref, wd_ref, o_ref):
    x = x_ref[0]
    ms = jnp.mean(x * x, axis=-1, keepdims=True)
    y = x * lax.rsqrt(ms + NORM_EPS) * g2_ref[...]
    h = (y * (1.0 + mod_ref[0, 4:5, :]) + mod_ref[0, 3:4, :]).astype(BF16)
    gate = _dot(h, wgu_ref[:, :FFN_HIDDEN])
    up = _dot(h, wgu_ref[:, FFN_HIDDEN:])
    act = (gate * _sigmoid(gate) * up).astype(BF16)
    o_ref[0] = x + mod_ref[0, 5:6, :] * _dot(act, wd_ref[...])


def _ffn(x, mod, g2, wgu, wd, tt):
    bsz, seq, d = x.shape

    def tile(width):
        return pl.BlockSpec((1, tt, width), lambda b, t: (b, t, 0))

    return pl.pallas_call(
        _ffn_kernel,
        grid=(bsz, seq // tt),
        in_specs=[tile(d), pl.BlockSpec((1, 6, d), lambda b, t: (b, 0, 0)),
                  _const_spec((1, d)), _const_spec(wgu.shape), _const_spec(wd.shape)],
        out_specs=tile(d),
        out_shape=jax.ShapeDtypeStruct((bsz, seq, d), F32),
        compiler_params=_params(2),
        name="ffn",
    )(x, mod, g2, wgu, wd)


def _rope_tables():
    half = ROT_DIM // 2
    inv_freq = ROPE_THETA ** (-np.arange(0, ROT_DIM, 2, dtype=np.float32) / ROT_DIM)
    tab = np.zeros((8, LANES), np.float32)
    for j in range(LANES):
        r = j % ATTN_HEAD_DIM
        if r < ROT_DIM:
            tab[0, j] = inv_freq[r % half]
            tab[1, j] = -1.0 if r < half else 0.0
            tab[2, j] = 1.0 if r >= half else 0.0
    return tab


def _pick_tile(seq, want):
    tt = min(want, seq)
    assert seq % tt == 0 and tt % ATTN_BLOCK == 0, (seq, tt)
    return tt


def kernel(x, c, positions, ada_w, ada_b, norm1_g, w_in, conv_w, q_norm_g, k_norm_g, sinks, a_log,
           dt_bias, dn_norm_g, w_branch, w_out, norm2_g, w_gate_up, w_down):
    bsz, seq, d = x.shape
    assert d == D_MODEL and ada_w.shape[0] == 1, "single-layer kernel"
    n_in = ATTN_Q_WIDTH + 2 * ATTN_KV_WIDTH + DN_CONV_DIM

    mod = _ada(c, ada_w[0], ada_b[0]).reshape(bsz, 6, d)

    w = w_in[0]
    w_packed = jnp.concatenate(
        [w[:, :n_in], w[:, n_in:n_in + 2 * DN_HEADS],
         jnp.zeros((d, LANES - 2 * DN_HEADS), w.dtype), w[:, n_in + 2 * DN_HEADS:]],
        axis=1).astype(BF16)
    assert w_packed.shape[1] == IN_PACKED

    head_id = np.arange(ATTN_Q_WIDTH) // ATTN_HEAD_DIM
    bd = jnp.asarray(head_id[:, None] == head_id[None, :], BF16)
    qkg = jnp.stack([jnp.tile(q_norm_g[0], ATTN_HEADS),
                     jnp.pad(jnp.tile(k_norm_g[0], ATTN_KV_HEADS), (0, ATTN_Q_WIDTH - ATTN_KV_WIDTH))])
    freq = jnp.asarray(_rope_tables())
    convw = conv_w[0].reshape(DN_CONV, DN_CONV_DIM)
    pad4 = (DN_HEADS, LANES - 2 * DN_HEADS)
    alog = jnp.pad(a_log[0], pad4).reshape(1, LANES)
    dtb = jnp.pad(dt_bias[0], pad4).reshape(1, LANES)
    sink_lanes = jnp.repeat(sinks[0], ATTN_HEAD_DIM).reshape(ATTN_HEADS // 2, LANES)

    q, k, v, dqkv, bg, zs, sg = _inproj(
        x, mod, positions.reshape(bsz, seq, 1), norm1_g[0].reshape(1, d), w_packed, bd, qkg, freq,
        convw, alog, dtb, _pick_tile(seq, 256))
    oa = _attn(q, k, v, sink_lanes, _pick_tile(seq, 512))
    od = _gdn(dqkv, bg, zs, dn_norm_g[0].reshape(1, DN_DIM), _pick_tile(seq, 256))
    x1 = _merge(x, oa, od, sg, mod, w_branch[0].astype(BF16), w_out[0].astype(BF16),
                _pick_tile(seq, 512))
    return _ffn(x1, mod, norm2_g[0].reshape(1, d), w_gate_up[0].astype(BF16),
                w_down[0].astype(BF16), _pick_tile(seq, 256))
```

```python
import functools

import jax
import jax.numpy as jnp
import numpy as np
from jax import lax
from jax.experimental import pallas as pl
from jax.experimental.pallas import tpu as pltpu

F32 = jnp.float32
BF16 = jnp.bfloat16

D_MODEL = 1024
ATTN_HEADS = 8
ATTN_KV_HEADS = 2
ATTN_HEAD_DIM = 64
ATTN_BLOCK = 128
ROT_DIM = ATTN_HEAD_DIM // 4
ROPE_THETA = 500000.0
ATTN_Q_WIDTH = ATTN_HEADS * ATTN_HEAD_DIM
ATTN_KV_WIDTH = ATTN_KV_HEADS * ATTN_HEAD_DIM
DN_HEADS = 4
DN_DIM = 128
DN_CONV = 4
DN_CHUNK = 64
DN_SUB = 8
DN_GROUP = 1
DN_WIDTH = DN_HEADS * DN_DIM
DN_CONV_DIM = 3 * DN_WIDTH
FFN_HIDDEN = 2816
NORM_EPS = 1e-6
LANES = 128

COL_QKV = 0
COL_DN = COL_QKV + ATTN_Q_WIDTH + 2 * ATTN_KV_WIDTH
COL_BA = COL_DN + DN_CONV_DIM
COL_Z = COL_BA + LANES
COL_GATES = COL_Z + DN_WIDTH
IN_PACKED = COL_GATES + 2 * D_MODEL

VMEM_LIMIT = 56 * 1024 * 1024


def _sigmoid(x):
    return 0.5 * jnp.tanh(0.5 * x) + 0.5


def _sigmoid_of_twice(xh):
    return 0.5 * jnp.tanh(xh) + 0.5


def _silu_of_twice(xh):
    return xh * jnp.tanh(xh) + xh


def _dot(a, b):
    return jnp.dot(a, b, preferred_element_type=F32)


def _dot_nt(a, b):
    return lax.dot_general(a, b, (((1,), (1,)), ((), ())), preferred_element_type=F32)


def _software_pipeline(stages):
    pending = None
    for issue, finish in stages:
        value = issue()
        if pending is not None:
            pending[0](pending[1])
        pending = (finish, value)
    pending[0](pending[1])


def _staged_pipeline(items, stages):
    depth = len(stages)
    values = {}
    for step in range(len(items) + depth - 1):
        for k in range(depth):
            i = step - k
            if 0 <= i < len(items):
                values[i] = stages[k](items[i], values.get(i))
                if k == depth - 1:
                    del values[i]


def _const_spec(shape):
    nd = len(shape)
    return pl.BlockSpec(shape, lambda *_: (0,) * nd, pipeline_mode=pl.Buffered(1))


def _params(n_grid):
    return pltpu.CompilerParams(dimension_semantics=("arbitrary",) * n_grid,
                                vmem_limit_bytes=VMEM_LIMIT)


def _ada_kernel(c_ref, w_ref, b_ref, o_ref):
    c = c_ref[...]
    cond = c * _sigmoid(c)
    o_ref[...] = jnp.dot(cond, w_ref[...], preferred_element_type=F32,
                         precision=lax.Precision.HIGHEST) + b_ref[...]


def _ada(c, w, b):
    bsz, d = c.shape
    n = w.shape[1]
    bn = 1536
    return pl.pallas_call(
        _ada_kernel,
        grid=(n // bn,),
        in_specs=[pl.BlockSpec((bsz, d), lambda j: (0, 0)),
                  pl.BlockSpec((d, bn), lambda j: (0, j)),
                  pl.BlockSpec((1, bn), lambda j: (0, j))],
        out_specs=pl.BlockSpec((bsz, bn), lambda j: (0, j)),
        out_shape=jax.ShapeDtypeStruct((bsz, n), F32),
        compiler_params=_params(1),
        name="ada",
    )(c, w, b.reshape(1, n))


def _inproj_kernel(x_ref, mod_ref, pos_ref, g1_ref, w_ref, bd_ref, qkg_ref, freq_ref, convw_ref,
                   alog_ref, dtb_ref,
                   q_ref, k_ref, v_ref, dqkv_ref, bg_ref, zs_ref, sg_ref,
                   xc_ref):
    t = pl.program_id(1)
    tt = x_ref.shape[1]

    x = x_ref[0]
    ms = jnp.mean(x * x, axis=-1, keepdims=True)
    shift1 = mod_ref[0, 0:1, :]
    gain_scale = g1_ref[...] * (1.0 + mod_ref[0, 1:2, :])
    h = (x * lax.rsqrt(ms + NORM_EPS) * gain_scale + shift1).astype(BF16)

    @pl.when(t == 0)
    def _():
        xc_ref[0:8, :] = jnp.zeros((8, DN_CONV_DIM), F32)

    ang = jnp.tile(freq_ref[...], (1, tt // LANES)) * pos_ref[0].astype(F32)
    cos_t = jnp.cos(ang)
    sin_t = jnp.sin(ang)
    one_t = jnp.ones_like(cos_t)
    zero_t = jnp.zeros_like(cos_t)

    def lane_table(first, second, fill):
        head = [first, second] + [fill] * (ATTN_HEAD_DIM // 8 - 2)
        return jnp.concatenate(head * (LANES // ATTN_HEAD_DIM), axis=0).T

    cos = lane_table(cos_t, cos_t, one_t)
    sin_signed = lane_table(-sin_t, sin_t, zero_t)
    half = ROT_DIM // 2
    first_half = (lax.broadcasted_iota(jnp.int32, (tt, LANES), 1) % ROT_DIM) < half

    def project(lo, hi):
        return lambda: _dot(h, w_ref[:, lo:hi])

    def finish_qkv(qkv):
        q = qkv[:, :ATTN_Q_WIDTH]
        k = qkv[:, ATTN_Q_WIDTH:ATTN_Q_WIDTH + ATTN_KV_WIDTH]
        v = qkv[:, ATTN_Q_WIDTH + ATTN_KV_WIDTH:]
        bd = bd_ref[...]
        ssq_q = _dot((q * q).astype(BF16), bd)
        ssq_k = _dot((k * k).astype(BF16), bd[:ATTN_KV_WIDTH, :ATTN_KV_WIDTH])
        qn = q * lax.rsqrt(ssq_q * (1.0 / ATTN_HEAD_DIM) + NORM_EPS) * qkg_ref[0:1, :]
        kn = k * lax.rsqrt(ssq_k * (1.0 / ATTN_HEAD_DIM) + NORM_EPS) * qkg_ref[1:2, :ATTN_KV_WIDTH]

        def rope(u):
            partner = jnp.where(first_half, pltpu.roll(u, LANES - half, axis=1), pltpu.roll(u, half, axis=1))
            return u * cos + partner * sin_signed

        for i in range(ATTN_Q_WIDTH // LANES):
            sl = slice(i * LANES, (i + 1) * LANES)
            q_ref[0, :, sl] = (rope(qn[:, sl]) * (ATTN_HEAD_DIM ** -0.5)).astype(BF16)
        k_ref[0] = rope(kn).astype(BF16)
        v_ref[0] = v.astype(BF16)

    def finish_conv(group):
        c0 = group * DN_WIDTH

        def finish(proj):
            cols = slice(c0, c0 + DN_WIDTH)
            xc_ref[8:tt + 8, cols] = proj
            conv = convw_ref[3:4, cols] * proj
            for i in range(DN_CONV - 1):
                off = 8 - (DN_CONV - 1) + i
                conv = conv + convw_ref[i:i + 1, cols] * xc_ref[off:off + tt, cols]
            xc_ref[0:8, cols] = xc_ref[tt:tt + 8, cols]
            act = _silu_of_twice(conv)
            if group == 2:
                dqkv_ref[0, :, cols] = act.astype(BF16)
                return
            scale = DN_DIM ** -0.5 if group == 0 else 1.0
            for i in range(DN_HEADS):
                u = act[:, i * DN_DIM:(i + 1) * DN_DIM]
                un = u * (lax.rsqrt(jnp.sum(u * u, axis=-1, keepdims=True) + NORM_EPS) * scale)
                dqkv_ref[0, :, c0 + i * DN_DIM:c0 + (i + 1) * DN_DIM] = un.astype(BF16)

        return finish

    def finish_ba_z(baz):
        ba = baz[:, :LANES]
        z = baz[:, LANES:]
        sp_in = ba + dtb_ref[...]
        softplus = jnp.maximum(sp_in, 0.0) + jnp.log(1.0 + jnp.exp(-jnp.abs(sp_in)))
        g = -jnp.exp(alog_ref[...]) * softplus
        lane = lax.broadcasted_iota(jnp.int32, ba.shape, 1)
        bg_ref[0] = jnp.where(lane < DN_HEADS, _sigmoid(ba), g)
        zs_ref[0] = _silu_of_twice(z).astype(BF16)

    def finish_gate(i):
        def finish(gate):
            sg_ref[0, :, i * D_MODEL:(i + 1) * D_MODEL] = _sigmoid_of_twice(gate).astype(BF16)
        return finish

    _software_pipeline(
        [(project(COL_QKV, COL_DN), finish_qkv)]
        + [(project(COL_DN + g * DN_WIDTH, COL_DN + (g + 1) * DN_WIDTH), finish_conv(g)) for g in range(3)]
        + [(project(COL_BA, COL_GATES), finish_ba_z)]
        + [(project(COL_GATES + i * D_MODEL, COL_GATES + (i + 1) * D_MODEL), finish_gate(i))
           for i in range(2)])


def _inproj(x, mod, pos, g1, w_packed, bd, qkg, freq, convw, alog, dtb, tt):
    bsz, seq, d = x.shape
    grid = (bsz, seq // tt)

    def tile(width):
        return pl.BlockSpec((1, tt, width), lambda b, t: (b, t, 0))

    out_shapes = [
        jax.ShapeDtypeStruct((bsz, seq, ATTN_Q_WIDTH), BF16),
        jax.ShapeDtypeStruct((bsz, seq, ATTN_KV_WIDTH), BF16),
        jax.ShapeDtypeStruct((bsz, seq, ATTN_KV_WIDTH), BF16),
        jax.ShapeDtypeStruct((bsz, seq, DN_CONV_DIM), BF16),
        jax.ShapeDtypeStruct((bsz, seq, LANES), F32),
        jax.ShapeDtypeStruct((bsz, seq, DN_WIDTH), BF16),
        jax.ShapeDtypeStruct((bsz, seq, 2 * D_MODEL), BF16),
    ]
    return pl.pallas_call(
        _inproj_kernel,
        grid=grid,
        in_specs=[tile(d),
                  pl.BlockSpec((1, 6, d), lambda b, t: (b, 0, 0)),
                  pl.BlockSpec((1, 1, tt), lambda b, t: (b, 0, t)),
                  _const_spec((1, d)),
                  _const_spec(w_packed.shape),
                  _const_spec(bd.shape),
                  _const_spec(qkg.shape),
                  _const_spec(freq.shape),
                  _const_spec(convw.shape),
                  _const_spec(alog.shape),
                  _const_spec(dtb.shape)],
        out_specs=[tile(ATTN_Q_WIDTH), tile(ATTN_KV_WIDTH), tile(ATTN_KV_WIDTH), tile(DN_CONV_DIM),
                   tile(LANES), tile(DN_WIDTH), tile(2 * D_MODEL)],
        out_shape=out_shapes,
        scratch_shapes=[pltpu.VMEM((tt + 8, DN_CONV_DIM), F32)],
        compiler_params=_params(2),
        name="inproj",
    )(x, mod, pos, g1, w_packed, bd, qkg, freq, convw, alog, dtb)


def _attn_kernel(q_ref, k_ref, v_ref, kp_ref, vp_ref, sink_ref, o_ref, kw_ref, vw_ref):
    t = pl.program_id(1)
    tt = q_ref.shape[1]
    blk = ATTN_BLOCK
    hd = ATTN_HEAD_DIM

    kw_ref[blk:, :] = k_ref[0]
    vw_ref[blk:, :] = v_ref[0]

    @pl.when(t == 0)
    def _():
        kw_ref[:blk, :] = jnp.zeros((blk, ATTN_KV_WIDTH), BF16)
        vw_ref[:blk, :] = jnp.zeros((blk, ATTN_KV_WIDTH), BF16)

    @pl.when(t > 0)
    def _():
        kw_ref[:blk, :] = kp_ref[0]
        vw_ref[:blk, :] = vp_ref[0]

    row = lax.broadcasted_iota(jnp.int32, (blk, 2 * blk), 0)
    col = lax.broadcasted_iota(jnp.int32, (blk, 2 * blk), 1)
    band = (col > row) & (col <= row + blk)
    win = tt + blk
    lo_kv = lax.broadcasted_iota(jnp.int32, (win, LANES), 1) < hd
    lo_q = lax.broadcasted_iota(jnp.int32, (blk, LANES), 1) < hd
    ones_lo = jnp.where(lo_kv, 1.0, 0.0).astype(BF16)
    ones_hi = jnp.where(lo_kv, 0.0, 1.0).astype(BF16)

    kwin = kw_ref[...].astype(F32)
    vwin = vw_ref[...].astype(F32)
    kswp = pltpu.roll(kwin, hd, axis=1)
    vswp = pltpu.roll(vwin, hd, axis=1)
    k_lo, k_hi, v_lo, v_hi = [], [], [], []
    for g in range(ATTN_KV_HEADS):
        k_lo.append(jnp.where(lo_kv, kwin if g == 0 else kswp, 0.0).astype(BF16))
        k_hi.append(jnp.where(lo_kv, 0.0, kswp if g == 0 else kwin).astype(BF16))
        v_lo.append(jnp.concatenate(
            [jnp.where(lo_kv, vwin if g == 0 else vswp, 0.0).astype(BF16), ones_lo], axis=1))
        v_hi.append(jnp.concatenate(
            [jnp.where(lo_kv, 0.0, vswp if g == 0 else vwin).astype(BF16), ones_hi], axis=1))

    valid_first = band & (col >= jnp.where(t == 0, blk, 0))
    items = [(j, p) for j in range(tt // blk) for p in range(ATTN_HEADS // 2)]

    def scores(item):
        j, p = item
        rows = slice(j * blk, (j + 2) * blk)
        return _dot_nt(q_ref[0, j * blk:(j + 1) * blk, p * LANES:(p + 1) * LANES],
                       jnp.concatenate([k_lo[p // 2][rows], k_hi[p // 2][rows]], axis=0))

    def softmax(item, s):
        j, p = item
        valid = valid_first if j == 0 else band
        ps, ms = [], []
        for e in range(2):
            se = jnp.where(valid, s[:, e * 2 * blk:(e + 1) * 2 * blk], -jnp.inf)
            m = jnp.maximum(jnp.max(se, axis=-1, keepdims=True), sink_ref[p:p + 1, e * hd:e * hd + 1])
            ps.append(jnp.exp(se - m).astype(BF16))
            ms.append(m)
        return jnp.concatenate(ps, axis=1), jnp.where(lo_q, ms[0], ms[1])

    def values(item, soft):
        j, p = item
        rows = slice(j * blk, (j + 2) * blk)
        pv = _dot(soft[0], jnp.concatenate([v_lo[p // 2][rows], v_hi[p // 2][rows]], axis=0))
        return pv, soft[1]

    def finish(item, res):
        j, p = item
        pv, m_pair = res
        denom = pv[:, LANES:] + jnp.exp(sink_ref[p:p + 1, :] - m_pair)
        o_ref[0, j * blk:(j + 1) * blk, p * LANES:(p + 1) * LANES] = (pv[:, :LANES] / denom).astype(BF16)

    _staged_pipeline(items, [lambda item, _: scores(item), softmax, values, finish])


def _attn(q, k, v, sink_lanes, tt):
    bsz, seq, _ = q.shape
    nprev = tt // ATTN_BLOCK

    def tile(width):
        return pl.BlockSpec((1, tt, width), lambda b, t: (b, t, 0))

    def prev(width):
        return pl.BlockSpec((1, ATTN_BLOCK, width),
                            lambda b, t: (b, jnp.maximum(t * nprev - 1, 0), 0))

    return pl.pallas_call(
        _attn_kernel,
        grid=(bsz, seq // tt),
        in_specs=[tile(ATTN_Q_WIDTH), tile(ATTN_KV_WIDTH), tile(ATTN_KV_WIDTH),
                  prev(ATTN_KV_WIDTH), prev(ATTN_KV_WIDTH),
                  _const_spec(sink_lanes.shape)],
        out_specs=tile(ATTN_Q_WIDTH),
        out_shape=jax.ShapeDtypeStruct((bsz, seq, ATTN_Q_WIDTH), BF16),
        scratch_shapes=[pltpu.VMEM((tt + ATTN_BLOCK, ATTN_KV_WIDTH), BF16),
                        pltpu.VMEM((tt + ATTN_BLOCK, ATTN_KV_WIDTH), BF16)],
        compiler_params=_params(2),
        name="attn",
    )(q, k, v, k, v, sink_lanes)


def _gdn_kernel(dqkv_ref, bg_ref, zs_ref, gain_ref, bcat_ref, o_ref, state_ref, gc_ref):
    t = pl.program_id(1)
    tt = dqkv_ref.shape[1]
    c = DN_CHUNK
    n_chunks = tt // c

    @pl.when(t == 0)
    def _():
        state_ref[...] = jnp.zeros(state_ref.shape, F32)

    bg = bg_ref[0]
    rin = lax.broadcasted_iota(jnp.int32, bg.shape, 0) % c
    gc = bg
    step = 1
    while step < c:
        gc = gc + jnp.where(rin >= step, pltpu.roll(gc, step, axis=0), 0.0)
        step *= 2
    gc_ref[...] = gc
    gct = gc.T

    ri = lax.broadcasted_iota(jnp.int32, (c, c), 0)
    ci = lax.broadcasted_iota(jnp.int32, (c, c), 1)
    incl = ri >= ci
    strict = ri > ci
    sb = DN_SUB
    diag_blocks = (ri // sb) == (ci // sb)
    merge_masks = []
    width = sb
    while width < c:
        merge_masks.append(((ri // (2 * width)) == (ci // (2 * width))) & ((ri // width) != (ci // width)))
        width *= 2
    lane_blk = lax.broadcasted_iota(jnp.int32, (sb, c), 1) // sb
    sub_p = lax.broadcasted_iota(jnp.int32, (sb, LANES), 0)
    lane_p = lax.broadcasted_iota(jnp.int32, (sb, LANES), 1)
    eye_packed = jnp.where(sub_p == lane_p % sb, 1.0, 0.0).astype(F32)

    heads = range(DN_HEADS)

    def load_and_gram(group, _):
        w = {}
        for ch in group:
            r0 = ch * c
            for h in heads:
                q = dqkv_ref[0, r0:r0 + c, h * DN_DIM:(h + 1) * DN_DIM].astype(F32)
                k = dqkv_ref[0, r0:r0 + c, DN_WIDTH + h * DN_DIM:DN_WIDTH + (h + 1) * DN_DIM].astype(F32)
                kb = k * bg_ref[0, r0:r0 + c, h:h + 1]
                a = _dot_nt(jnp.concatenate([kb, q], axis=0).astype(BF16), k.astype(BF16))
                w[ch, h] = dict(q=q, k=k, kb=kb, a=a)
        return w

    def intra_chunk(group, w):
        packed = []
        for ch, h in [(ch, h) for ch in group for h in heads]:
            r0 = ch * c
            wk = w[ch, h]
            v = dqkv_ref[0, r0:r0 + c, 2 * DN_WIDTH + h * DN_DIM:2 * DN_WIDTH + (h + 1) * DN_DIM].astype(F32)
            beta = bg_ref[0, r0:r0 + c, h:h + 1]
            gcc = gc_ref[r0:r0 + c, DN_HEADS + h:DN_HEADS + h + 1]
            gcr = gct[DN_HEADS + h:DN_HEADS + h + 1, r0:r0 + c]
            g_last = gcc[c - 1:c, :]
            eg = jnp.exp(gcc)
            decay = jnp.exp(jnp.where(incl, gcc - gcr, -jnp.inf))
            l_mat = jnp.where(strict, wk["a"][:c] * decay, 0.0)
            d8 = jnp.zeros((sb, c), F32)
            for g in range(c // sb):
                d8 = d8 + jnp.where(lane_blk == g, l_mat[g * sb:(g + 1) * sb, :], 0.0)
            packed.append(d8)
            w[ch, h] = dict(
                l=l_mat, a_intra=(wk["a"][c:] * decay).astype(BF16),
                rhs=jnp.concatenate([v * beta, wk["kb"] * eg], axis=1).astype(BF16),
                q_dec=wk["q"] * eg, k_dec_t=(wk["k"] * jnp.exp(g_last - gcc)).T.astype(BF16),
                d_chunk=jnp.exp(g_last))
        pall = jnp.concatenate(
            [jnp.concatenate(packed[2 * i:2 * i + 2], axis=1) for i in range(len(packed) // 2)], axis=0)
        w["packed"] = pall
        p_hi = pall.astype(BF16)
        p_lo = (pall - p_hi.astype(F32)).astype(BF16)
        spread = _dot(p_hi, bcat_ref[...]) + _dot(p_lo, bcat_ref[...])
        w["cols"] = [spread[:, j * LANES:(j + 1) * LANES] for j in range(sb - 1)]
        return w

    def diag_inverse(group, w):
        pall = w["packed"]
        n_pairs = pall.shape[0] // sb
        cols = w["cols"]
        tp = jnp.concatenate([eye_packed] * n_pairs, axis=0)
        for j in range(sb - 1):
            row_j = jnp.concatenate(
                [jnp.broadcast_to(tp[p * sb + j:p * sb + j + 1, :], (sb, LANES)) for p in range(n_pairs)], axis=0)
            tp = tp - cols[j] * row_j
        for i, key in enumerate([(ch, h) for ch in group for h in heads]):
            blocks = tp[(i // 2) * sb:(i // 2 + 1) * sb, (i % 2) * c:(i % 2 + 1) * c]
            w[key]["tinv"] = jnp.where(diag_blocks, jnp.tile(blocks, (c // sb, 1)), 0.0)
        return w

    def lower_rows(mat, width):
        return jnp.concatenate([mat[b * 2 * width + width:(b + 1) * 2 * width] for b in range(c // (2 * width))],
                               axis=0)

    def per_key(fn):
        def stage(group, w):
            for key in [(ch, h) for ch in group for h in heads]:
                fn(w[key])
            return w
        return stage

    def merge_first(mask, width):
        def fn(wk):
            wk["tb"] = wk["tinv"].astype(BF16)
            wk["half"] = _dot(lower_rows(wk["tinv"], width).astype(BF16),
                              jnp.where(mask, wk["l"], 0.0).astype(BF16)).astype(BF16)
        return per_key(fn)

    def merge_second(width):
        def fn(wk):
            t = wk["tinv"]
            new_low = lower_rows(t, width) - _dot(wk["half"], wk["tb"])
            parts = []
            for b in range(c // (2 * width)):
                parts += [t[b * 2 * width:b * 2 * width + width], new_low[b * width:(b + 1) * width]]
            wk["tinv"] = jnp.concatenate(parts, axis=0)
        return per_key(fn)

    def solve(wk):
        wk["uw"] = _dot(wk["tinv"].astype(BF16), wk["rhs"]).astype(BF16)

    def state_free(wk):
        both = _dot(jnp.concatenate([wk["a_intra"], wk["k_dec_t"]], axis=0), wk["uw"])
        wk["a_uw"] = both[:c]
        wk["k_uw"] = both[c:]
        wk["lhs"] = jnp.concatenate([wk["q_dec"] - both[:c, DN_DIM:], both[c:, DN_DIM:]],
                                    axis=0).astype(BF16)

    state = [state_ref[h] for h in heads]

    def recurrence(group, w):
        for ch in group:
            r0 = ch * c
            prod = {h: _dot(w[ch, h]["lhs"], state[h].astype(BF16)) for h in heads}
            for h in heads:
                wk = w[ch, h]
                o = prod[h][:c] + wk["a_uw"][:, :DN_DIM]
                state[h] = state[h] * wk["d_chunk"] - prod[h][c:] + wk["k_uw"][:, :DN_DIM]
                on = o * lax.rsqrt(jnp.mean(o * o, axis=-1, keepdims=True) + NORM_EPS) * gain_ref[...]
                zs = zs_ref[0, r0:r0 + c, h * DN_DIM:(h + 1) * DN_DIM].astype(F32)
                o_ref[0, r0:r0 + c, h * DN_DIM:(h + 1) * DN_DIM] = (on * zs).astype(BF16)

    stages = [load_and_gram, intra_chunk, diag_inverse]
    for level, mask in enumerate(merge_masks):
        stages += [merge_first(mask, sb << level), merge_second(sb << level)]
    stages += [per_key(solve), per_key(state_free), recurrence]
    group_size = min(DN_GROUP, n_chunks)
    groups = [list(range(g, g + group_size)) for g in range(0, n_chunks, group_size)]
    _staged_pipeline(groups, stages)
    for h in heads:
        state_ref[h] = state[h]


def _gdn_spread_matrix():
    sb = DN_SUB
    mat = np.zeros((LANES, (sb - 1) * LANES), np.float32)
    for j in range(sb - 1):
        for m in range(LANES // sb):
            mat[sb * m + j, j * LANES + sb * m:j * LANES + sb * (m + 1)] = 1.0
    return mat


def _gdn(dqkv, bg, zs, gain, tt):
    bsz, seq, _ = dqkv.shape

    def tile(width):
        return pl.BlockSpec((1, tt, width), lambda b, t: (b, t, 0))

    bcat = jnp.asarray(_gdn_spread_matrix(), BF16)
    return pl.pallas_call(
        _gdn_kernel,
        grid=(bsz, seq // tt),
        in_specs=[tile(DN_CONV_DIM), tile(LANES), tile(DN_WIDTH), _const_spec(gain.shape),
                  _const_spec(bcat.shape)],
        out_specs=tile(DN_WIDTH),
        out_shape=jax.ShapeDtypeStruct((bsz, seq, DN_WIDTH), BF16),
        scratch_shapes=[pltpu.VMEM((DN_HEADS, DN_DIM, DN_DIM), F32),
                        pltpu.VMEM((tt, LANES), F32)],
        compiler_params=_params(2),
        name="gdn",
    )(dqkv, bg, zs, gain, bcat)


def _merge_kernel(x_ref, oa_ref, od_ref, sg_ref, mod_ref, wb_ref, wo_ref, o_ref):
    ya = _dot(oa_ref[0], wb_ref[:ATTN_Q_WIDTH, :])
    yd = _dot(od_ref[0], wb_ref[ATTN_Q_WIDTH:, :])
    merged = sg_ref[0, :, :D_MODEL].astype(F32) * ya + sg_ref[0, :, D_MODEL:].astype(F32) * yd
    out = _dot(merged.astype(BF16), wo_ref[...])
    o_ref[0] = x_ref[0] + mod_ref[0, 2:3, :] * out


def _merge(x, oa, od, sg, mod, wb, wo, tt):
    bsz, seq, d = x.shape

    def tile(width):
        return pl.BlockSpec((1, tt, width), lambda b, t: (b, t, 0))

    return pl.pallas_call(
        _merge_kernel,
        grid=(bsz, seq // tt),
        in_specs=[tile(d), tile(ATTN_Q_WIDTH), tile(DN_WIDTH), tile(2 * d),
                  pl.BlockSpec((1, 6, d), lambda b, t: (b, 0, 0)),
                  _const_spec(wb.shape), _const_spec(wo.shape)],
        out_specs=tile(d),
        out_shape=jax.ShapeDtypeStruct((bsz, seq, d), F32),
        compiler_params=_params(2),
        name="merge",
    )(x, oa, od, sg, mod, wb, wo)


def _ffn_kernel(x_ref, mod_ref, g2_ref, wgu_ref, wd_ref, o_ref):
    x = x_ref[0]
    ms = jnp.mean(x * x, axis=-1, keepdims=True)
    gain_scale = g2_ref[...] * (1.0 + mod_ref[0, 4:5, :])
    h = (x * lax.rsqrt(ms + NORM_EPS) * gain_scale + mod_ref[0, 3:4, :]).astype(BF16)
    gate = _dot(h, wgu_ref[:, :FFN_HIDDEN])
    up = _dot(h, wgu_ref[:, FFN_HIDDEN:])
    act = (_silu_of_twice(gate) * up).astype(BF16)
    o_ref[0] = x + mod_ref[0, 5:6, :] * _dot(act, wd_ref[...])


def _ffn(x, mod, g2, wgu, wd, tt):
    bsz, seq, d = x.shape

    def tile(width):
        return pl.BlockSpec((1, tt, width), lambda b, t: (b, t, 0))

    return pl.pallas_call(
        _ffn_kernel,
        grid=(bsz, seq // tt),
        in_specs=[tile(d), pl.BlockSpec((1, 6, d), lambda b, t: (b, 0, 0)),
                  _const_spec((1, d)), _const_spec(wgu.shape), _const_spec(wd.shape)],
        out_specs=tile(d),
        out_shape=jax.ShapeDtypeStruct((bsz, seq, d), F32),
        compiler_params=_params(2),
        name="ffn",
    )(x, mod, g2, wgu, wd)


def _rope_freqs():
    inv_freq = ROPE_THETA ** (-jnp.arange(0, ROT_DIM, 2, dtype=F32) / ROT_DIM)
    return jnp.broadcast_to(inv_freq[:, None], (ROT_DIM // 2, LANES))


def _pick_tile(seq, want):
    tt = min(want, seq)
    assert seq % tt == 0 and tt % ATTN_BLOCK == 0, (seq, tt)
    return tt


def kernel(x, c, positions, ada_w, ada_b, norm1_g, w_in, conv_w, q_norm_g, k_norm_g, sinks, a_log,
           dt_bias, dn_norm_g, w_branch, w_out, norm2_g, w_gate_up, w_down):
    bsz, seq, d = x.shape
    assert d == D_MODEL and ada_w.shape[0] == 1, "single-layer kernel"
    n_in = ATTN_Q_WIDTH + 2 * ATTN_KV_WIDTH + DN_CONV_DIM

    mod = _ada(c, ada_w[0], ada_b[0]).reshape(bsz, 6, d)

    w = w_in[0]
    w_packed = jnp.concatenate(
        [w[:, :n_in], w[:, n_in:n_in + 2 * DN_HEADS],
         jnp.zeros((d, LANES - 2 * DN_HEADS), w.dtype), 0.5 * w[:, n_in + 2 * DN_HEADS:]],
        axis=1).astype(BF16)
    assert w_packed.shape[1] == IN_PACKED
    wgu = jnp.concatenate([0.5 * w_gate_up[0][:, :FFN_HIDDEN], w_gate_up[0][:, FFN_HIDDEN:]],
                          axis=1).astype(BF16)

    head_id = np.arange(ATTN_Q_WIDTH) // ATTN_HEAD_DIM
    bd = jnp.asarray(head_id[:, None] == head_id[None, :], BF16)
    qkg = jnp.stack([jnp.tile(q_norm_g[0], ATTN_HEADS),
                     jnp.pad(jnp.tile(k_norm_g[0], ATTN_KV_HEADS), (0, ATTN_Q_WIDTH - ATTN_KV_WIDTH))])
    freq = _rope_freqs()
    convw = 0.5 * conv_w[0].reshape(DN_CONV, DN_CONV_DIM)
    pad4 = (DN_HEADS, LANES - 2 * DN_HEADS)
    alog = jnp.pad(a_log[0], pad4).reshape(1, LANES)
    dtb = jnp.pad(dt_bias[0], pad4).reshape(1, LANES)
    sink_lanes = jnp.repeat(sinks[0], ATTN_HEAD_DIM).reshape(ATTN_HEADS // 2, LANES)

    q, k, v, dqkv, bg, zs, sg = _inproj(
        x, mod, positions.reshape(bsz, 1, seq), norm1_g[0].reshape(1, d), w_packed, bd, qkg, freq,
        convw, alog, dtb, _pick_tile(seq, 256))
    oa = _attn(q, k, v, sink_lanes, _pick_tile(seq, 512))
    od = _gdn(dqkv, bg, zs, dn_norm_g[0].reshape(1, DN_DIM), _pick_tile(seq, 512))
    x1 = _merge(x, oa, od, sg, mod, w_branch[0].astype(BF16), w_out[0].astype(BF16),
                _pick_tile(seq, 512))
    return _ffn(x1, mod, norm2_g[0].reshape(1, d), wgu, w_down[0].astype(BF16), _pick_tile(seq, 256))
```

```python
import functools

import jax
import jax.numpy as jnp
import numpy as np
from jax import lax
from jax.experimental import pallas as pl
from jax.experimental.pallas import tpu as pltpu

F32 = jnp.float32
BF16 = jnp.bfloat16

D_MODEL = 1024
ATTN_HEADS = 8
ATTN_KV_HEADS = 2
ATTN_HEAD_DIM = 64
ATTN_BLOCK = 128
ROT_DIM = ATTN_HEAD_DIM // 4
ROPE_THETA = 500000.0
ATTN_Q_WIDTH = ATTN_HEADS * ATTN_HEAD_DIM
ATTN_KV_WIDTH = ATTN_KV_HEADS * ATTN_HEAD_DIM
DN_HEADS = 4
DN_DIM = 128
DN_CONV = 4
DN_CHUNK = 64
DN_SUB = 8
DN_GROUP = 2
DN_SEQS = 2
DN_WIDTH = DN_HEADS * DN_DIM
DN_CONV_DIM = 3 * DN_WIDTH
FFN_HIDDEN = 2816
FFN_SPLIT = 1536
NORM_EPS = 1e-6
LANES = 128

COL_QKV = 0
COL_DN = COL_QKV + ATTN_Q_WIDTH + 2 * ATTN_KV_WIDTH
COL_BA = COL_DN + DN_CONV_DIM
COL_Z = COL_BA + LANES
COL_GATES = COL_Z + DN_WIDTH
IN_PACKED = COL_GATES + 2 * D_MODEL

VMEM_LIMIT = 56 * 1024 * 1024


def _sigmoid(x):
    return 0.5 * jnp.tanh(0.5 * x) + 0.5


def _sigmoid_of_twice(xh):
    return 0.5 * jnp.tanh(xh) + 0.5


def _silu_of_twice(xh):
    return xh * jnp.tanh(xh) + xh


def _dot(a, b):
    return jnp.dot(a, b, preferred_element_type=F32)


def _dot_nt(a, b):
    return lax.dot_general(a, b, (((1,), (1,)), ((), ())), preferred_element_type=F32)


def _software_pipeline(stages):
    pending = None
    for issue, finish in stages:
        value = issue()
        if pending is not None:
            pending[0](pending[1])
        pending = (finish, value)
    pending[0](pending[1])


def _staged_pipeline(items, stages):
    depth = len(stages)
    values = {}
    for step in range(len(items) + depth - 1):
        for k in range(depth):
            i = step - k
            if 0 <= i < len(items):
                values[i] = stages[k](items[i], values.get(i))
                if k == depth - 1:
                    del values[i]


def _const_spec(shape):
    nd = len(shape)
    return pl.BlockSpec(shape, lambda *_: (0,) * nd, pipeline_mode=pl.Buffered(1))


def _params(n_grid):
    return pltpu.CompilerParams(dimension_semantics=("arbitrary",) * n_grid,
                                vmem_limit_bytes=VMEM_LIMIT)


def _ada_kernel(c_ref, w_ref, b_ref, o_ref):
    c = c_ref[...]
    cond = c * _sigmoid(c)
    o_ref[...] = jnp.dot(cond, w_ref[...], preferred_element_type=F32,
                         precision=lax.Precision.HIGHEST) + b_ref[...]


def _ada(c, w, b):
    bsz, d = c.shape
    n = w.shape[1]
    bn = 1536
    return pl.pallas_call(
        _ada_kernel,
        grid=(n // bn,),
        in_specs=[pl.BlockSpec((bsz, d), lambda j: (0, 0)),
                  pl.BlockSpec((d, bn), lambda j: (0, j)),
                  pl.BlockSpec((1, bn), lambda j: (0, j))],
        out_specs=pl.BlockSpec((bsz, bn), lambda j: (0, j)),
        out_shape=jax.ShapeDtypeStruct((bsz, n), F32),
        compiler_params=_params(1),
        name="ada",
    )(c, w, b.reshape(1, n))


def _inproj_kernel(x_ref, mod_ref, pos_ref, g1_ref, w_ref, bd_ref, qkg_ref, freq_ref, convw_ref,
                   alog_ref, dtb_ref,
                   q_ref, k_ref, v_ref, dqkv_ref, bg_ref, zs_ref, sg_ref,
                   xc_ref):
    t = pl.program_id(1)
    tt = x_ref.shape[1]

    x = x_ref[0]
    ms = jnp.mean(x * x, axis=-1, keepdims=True)
    shift1 = mod_ref[0, 0:1, :]
    gain_scale = g1_ref[...] * (1.0 + mod_ref[0, 1:2, :])
    h = (x * lax.rsqrt(ms + NORM_EPS) * gain_scale + shift1).astype(BF16)

    @pl.when(t == 0)
    def _():
        xc_ref[0:8, :] = jnp.zeros((8, DN_CONV_DIM), F32)

    ang = jnp.tile(freq_ref[...], (1, tt // LANES)) * pos_ref[0].astype(F32)
    cos_t = jnp.cos(ang)
    sin_t = jnp.sin(ang)
    one_t = jnp.ones_like(cos_t)
    zero_t = jnp.zeros_like(cos_t)

    def lane_table(first, second, fill):
        head = [first, second] + [fill] * (ATTN_HEAD_DIM // 8 - 2)
        return jnp.concatenate(head * (LANES // ATTN_HEAD_DIM), axis=0).T

    cos = lane_table(cos_t, cos_t, one_t)
    sin_signed = lane_table(-sin_t, sin_t, zero_t)
    half = ROT_DIM // 2
    first_half = (lax.broadcasted_iota(jnp.int32, (tt, LANES), 1) % ROT_DIM) < half

    def project(lo, hi):
        return lambda: _dot(h, w_ref[:, lo:hi])

    def finish_qkv(qkv):
        q = qkv[:, :ATTN_Q_WIDTH]
        k = qkv[:, ATTN_Q_WIDTH:ATTN_Q_WIDTH + ATTN_KV_WIDTH]
        v = qkv[:, ATTN_Q_WIDTH + ATTN_KV_WIDTH:]
        bd = bd_ref[...]
        ssq_q = _dot((q * q).astype(BF16), bd)
        ssq_k = _dot((k * k).astype(BF16), bd[:ATTN_KV_WIDTH, :ATTN_KV_WIDTH])
        qn = q * lax.rsqrt(ssq_q * (1.0 / ATTN_HEAD_DIM) + NORM_EPS) * qkg_ref[0:1, :]
        kn = k * lax.rsqrt(ssq_k * (1.0 / ATTN_HEAD_DIM) + NORM_EPS) * qkg_ref[1:2, :ATTN_KV_WIDTH]

        def rope(u):
            partner = jnp.where(first_half, pltpu.roll(u, LANES - half, axis=1), pltpu.roll(u, half, axis=1))
            return u * cos + partner * sin_signed

        for i in range(ATTN_Q_WIDTH // LANES):
            sl = slice(i * LANES, (i + 1) * LANES)
            q_ref[0, :, sl] = (rope(qn[:, sl]) * (ATTN_HEAD_DIM ** -0.5)).astype(BF16)
        k_ref[0] = rope(kn).astype(BF16)
        v_ref[0] = v.astype(BF16)

    def finish_conv(group):
        c0 = group * DN_WIDTH

        def finish(proj):
            cols = slice(c0, c0 + DN_WIDTH)
            xc_ref[8:tt + 8, cols] = proj
            conv = convw_ref[3:4, cols] * proj
            for i in range(DN_CONV - 1):
                off = 8 - (DN_CONV - 1) + i
                conv = conv + convw_ref[i:i + 1, cols] * xc_ref[off:off + tt, cols]
            xc_ref[0:8, cols] = xc_ref[tt:tt + 8, cols]
            act = _silu_of_twice(conv)
            if group == 2:
                dqkv_ref[0, :, cols] = act.astype(BF16)
                return
            scale = DN_DIM ** -0.5 if group == 0 else 1.0
            for i in range(DN_HEADS):
                u = act[:, i * DN_DIM:(i + 1) * DN_DIM]
                un = u * (lax.rsqrt(jnp.sum(u * u, axis=-1, keepdims=True) + NORM_EPS) * scale)
                dqkv_ref[0, :, c0 + i * DN_DIM:c0 + (i + 1) * DN_DIM] = un.astype(BF16)

        return finish

    def finish_ba_z(baz):
        ba = baz[:, :LANES]
        z = baz[:, LANES:]
        sp_in = ba + dtb_ref[...]
        softplus = jnp.maximum(sp_in, 0.0) + jnp.log(1.0 + jnp.exp(-jnp.abs(sp_in)))
        g = -jnp.exp(alog_ref[...]) * softplus
        lane = lax.broadcasted_iota(jnp.int32, ba.shape, 1)
        bg_ref[0] = jnp.where(lane < DN_HEADS, _sigmoid(ba), g)
        zs_ref[0] = _silu_of_twice(z).astype(BF16)

    def finish_gate(i):
        def finish(gate):
            sg_ref[0, :, i * D_MODEL:(i + 1) * D_MODEL] = gate.astype(BF16)
        return finish

    _software_pipeline(
        [(project(COL_QKV, COL_DN), finish_qkv)]
        + [(project(COL_DN + g * DN_WIDTH, COL_DN + (g + 1) * DN_WIDTH), finish_conv(g)) for g in range(3)]
        + [(project(COL_BA, COL_GATES), finish_ba_z)]
        + [(project(COL_GATES + i * D_MODEL, COL_GATES + (i + 1) * D_MODEL), finish_gate(i))
           for i in range(2)])


def _inproj(x, mod, pos, g1, w_packed, bd, qkg, freq, convw, alog, dtb, tt):
    bsz, seq, d = x.shape
    grid = (bsz, seq // tt)

    def tile(width):
        return pl.BlockSpec((1, tt, width), lambda b, t: (b, t, 0))

    out_shapes = [
        jax.ShapeDtypeStruct((bsz, seq, ATTN_Q_WIDTH), BF16),
        jax.ShapeDtypeStruct((bsz, seq, ATTN_KV_WIDTH), BF16),
        jax.ShapeDtypeStruct((bsz, seq, ATTN_KV_WIDTH), BF16),
        jax.ShapeDtypeStruct((bsz, seq, DN_CONV_DIM), BF16),
        jax.ShapeDtypeStruct((bsz, seq, LANES), F32),
        jax.ShapeDtypeStruct((bsz, seq, DN_WIDTH), BF16),
        jax.ShapeDtypeStruct((bsz, seq, 2 * D_MODEL), BF16),
    ]
    return pl.pallas_call(
        _inproj_kernel,
        grid=grid,
        in_specs=[tile(d),
                  pl.BlockSpec((1, 6, d), lambda b, t: (b, 0, 0)),
                  pl.BlockSpec((1, 1, tt), lambda b, t: (b, 0, t)),
                  _const_spec((1, d)),
                  _const_spec(w_packed.shape),
                  _const_spec(bd.shape),
                  _const_spec(qkg.shape),
                  _const_spec(freq.shape),
                  _const_spec(convw.shape),
                  _const_spec(alog.shape),
                  _const_spec(dtb.shape)],
        out_specs=[tile(ATTN_Q_WIDTH), tile(ATTN_KV_WIDTH), tile(ATTN_KV_WIDTH), tile(DN_CONV_DIM),
                   tile(LANES), tile(DN_WIDTH), tile(2 * D_MODEL)],
        out_shape=out_shapes,
        scratch_shapes=[pltpu.VMEM((tt + 8, DN_CONV_DIM), F32)],
        compiler_params=_params(2),
        name="inproj",
    )(x, mod, pos, g1, w_packed, bd, qkg, freq, convw, alog, dtb)


def _attn_kernel(q_ref, k_ref, v_ref, kp_ref, vp_ref, sink_ref, o_ref, kw_ref, vw_ref):
    t = pl.program_id(1)
    tt = q_ref.shape[1]
    blk = ATTN_BLOCK
    hd = ATTN_HEAD_DIM

    kw_ref[blk:, :] = k_ref[0]
    vw_ref[blk:, :] = v_ref[0]

    @pl.when(t == 0)
    def _():
        kw_ref[:blk, :] = jnp.zeros((blk, ATTN_KV_WIDTH), BF16)
        vw_ref[:blk, :] = jnp.zeros((blk, ATTN_KV_WIDTH), BF16)

    @pl.when(t > 0)
    def _():
        kw_ref[:blk, :] = kp_ref[0]
        vw_ref[:blk, :] = vp_ref[0]

    row = lax.broadcasted_iota(jnp.int32, (blk, 2 * blk), 0)
    col = lax.broadcasted_iota(jnp.int32, (blk, 2 * blk), 1)
    band = (col > row) & (col <= row + blk)
    win = tt + blk
    lo_kv = lax.broadcasted_iota(jnp.int32, (win, LANES), 1) < hd
    lo_q = lax.broadcasted_iota(jnp.int32, (blk, LANES), 1) < hd
    ones_lo = jnp.where(lo_kv, 1.0, 0.0).astype(BF16)
    ones_hi = jnp.where(lo_kv, 0.0, 1.0).astype(BF16)

    kwin = kw_ref[...].astype(F32)
    vwin = vw_ref[...].astype(F32)
    kswp = pltpu.roll(kwin, hd, axis=1)
    vswp = pltpu.roll(vwin, hd, axis=1)
    k_lo, k_hi, v_lo, v_hi = [], [], [], []
    for g in range(ATTN_KV_HEADS):
        k_lo.append(jnp.where(lo_kv, kwin if g == 0 else kswp, 0.0).astype(BF16))
        k_hi.append(jnp.where(lo_kv, 0.0, kswp if g == 0 else kwin).astype(BF16))
        v_lo.append(jnp.concatenate(
            [jnp.where(lo_kv, vwin if g == 0 else vswp, 0.0).astype(BF16), ones_lo], axis=1))
        v_hi.append(jnp.concatenate(
            [jnp.where(lo_kv, 0.0, vswp if g == 0 else vwin).astype(BF16), ones_hi], axis=1))

    valid_first = band & (col >= jnp.where(t == 0, blk, 0))
    items = [(j, p) for j in range(tt // blk) for p in range(ATTN_HEADS // 2)]

    def scores(item, _):
        j, p = item
        rows = slice(j * blk, (j + 2) * blk)
        return _dot_nt(q_ref[0, j * blk:(j + 1) * blk, p * LANES:(p + 1) * LANES],
                       jnp.concatenate([k_lo[p // 2][rows], k_hi[p // 2][rows]], axis=0))

    def softmax(item, s):
        j, p = item
        valid = valid_first if j == 0 else band
        ps, ms = [], []
        for e in range(2):
            se = jnp.where(valid, s[:, e * 2 * blk:(e + 1) * 2 * blk], -jnp.inf)
            m = jnp.maximum(jnp.max(se, axis=-1, keepdims=True), sink_ref[p:p + 1, e * hd:e * hd + 1])
            ps.append(jnp.exp(se - m).astype(BF16))
            ms.append(m)
        return jnp.concatenate(ps, axis=1), jnp.where(lo_q, ms[0], ms[1])

    def values(item, soft):
        j, p = item
        rows = slice(j * blk, (j + 2) * blk)
        pv = _dot(soft[0], jnp.concatenate([v_lo[p // 2][rows], v_hi[p // 2][rows]], axis=0))
        return pv, soft[1]

    def finish(item, res):
        j, p = item
        pv, m_pair = res
        denom = pv[:, LANES:] + jnp.exp(sink_ref[p:p + 1, :] - m_pair)
        o_ref[0, j * blk:(j + 1) * blk, p * LANES:(p + 1) * LANES] = (pv[:, :LANES] / denom).astype(BF16)

    _staged_pipeline(items, [scores, softmax, values, finish])


def _attn(q, k, v, sink_lanes, tt):
    bsz, seq, _ = q.shape
    nprev = tt // ATTN_BLOCK

    def tile(width):
        return pl.BlockSpec((1, tt, width), lambda b, t: (b, t, 0))

    def prev(width):
        return pl.BlockSpec((1, ATTN_BLOCK, width),
                            lambda b, t: (b, jnp.maximum(t * nprev - 1, 0), 0))

    return pl.pallas_call(
        _attn_kernel,
        grid=(bsz, seq // tt),
        in_specs=[tile(ATTN_Q_WIDTH), tile(ATTN_KV_WIDTH), tile(ATTN_KV_WIDTH),
                  prev(ATTN_KV_WIDTH), prev(ATTN_KV_WIDTH),
                  _const_spec(sink_lanes.shape)],
        out_specs=tile(ATTN_Q_WIDTH),
        out_shape=jax.ShapeDtypeStruct((bsz, seq, ATTN_Q_WIDTH), BF16),
        scratch_shapes=[pltpu.VMEM((tt + ATTN_BLOCK, ATTN_KV_WIDTH), BF16),
                        pltpu.VMEM((tt + ATTN_BLOCK, ATTN_KV_WIDTH), BF16)],
        compiler_params=_params(2),
        name="attn",
    )(q, k, v, k, v, sink_lanes)


def _gdn_kernel(dqkv_ref, bg_ref, zs_ref, gain_ref, bcat_ref, o_ref, state_ref, gc_ref):
    t = pl.program_id(1)
    nb, tt = dqkv_ref.shape[0], dqkv_ref.shape[1]
    c = DN_CHUNK
    n_chunks = tt // c

    @pl.when(t == 0)
    def _():
        state_ref[...] = jnp.zeros(state_ref.shape, F32)

    rin = lax.broadcasted_iota(jnp.int32, (tt, LANES), 0) % c
    gcts = []
    for bi in range(nb):
        gc = bg_ref[bi]
        step = 1
        while step < c:
            gc = gc + jnp.where(rin >= step, pltpu.roll(gc, step, axis=0), 0.0)
            step *= 2
        gc_ref[bi] = gc
        gcts.append(gc.T)

    def locate(ch):
        return ch % nb, (ch // nb) * c

    ri = lax.broadcasted_iota(jnp.int32, (c, c), 0)
    ci = lax.broadcasted_iota(jnp.int32, (c, c), 1)
    incl = ri >= ci
    strict = ri > ci
    sb = DN_SUB
    diag_blocks = (ri // sb) == (ci // sb)
    merge_masks = []
    width = sb
    while width < c:
        merge_masks.append(((ri // (2 * width)) == (ci // (2 * width))) & ((ri // width) != (ci // width)))
        width *= 2
    lane_blk = lax.broadcasted_iota(jnp.int32, (sb, c), 1) // sb
    sub_p = lax.broadcasted_iota(jnp.int32, (sb, LANES), 0)
    lane_p = lax.broadcasted_iota(jnp.int32, (sb, LANES), 1)
    eye_packed = jnp.where(sub_p == lane_p % sb, 1.0, 0.0).astype(F32)

    heads = range(DN_HEADS)

    def load_and_gram(group, _):
        w = {}
        for ch in group:
            bi, r0 = locate(ch)
            for h in heads:
                q = dqkv_ref[bi, r0:r0 + c, h * DN_DIM:(h + 1) * DN_DIM].astype(F32)
                k = dqkv_ref[bi, r0:r0 + c, DN_WIDTH + h * DN_DIM:DN_WIDTH + (h + 1) * DN_DIM].astype(F32)
                kb = k * bg_ref[bi, r0:r0 + c, h:h + 1]
                a = _dot_nt(jnp.concatenate([kb, q], axis=0).astype(BF16), k.astype(BF16))
                w[ch, h] = dict(q=q, k=k, kb=kb, a=a)
        return w

    def intra_chunk(group, w):
        packed = []
        for ch, h in [(ch, h) for ch in group for h in heads]:
            bi, r0 = locate(ch)
            wk = w[ch, h]
            v = dqkv_ref[bi, r0:r0 + c, 2 * DN_WIDTH + h * DN_DIM:2 * DN_WIDTH + (h + 1) * DN_DIM].astype(F32)
            beta = bg_ref[bi, r0:r0 + c, h:h + 1]
            gcc = gc_ref[bi, r0:r0 + c, DN_HEADS + h:DN_HEADS + h + 1]
            gcr = gcts[bi][DN_HEADS + h:DN_HEADS + h + 1, r0:r0 + c]
            g_last = gcc[c - 1:c, :]
            eg = jnp.exp(gcc)
            decay = jnp.exp(jnp.where(incl, gcc - gcr, -jnp.inf))
            l_mat = jnp.where(strict, wk["a"][:c] * decay, 0.0)
            d8 = jnp.zeros((sb, c), F32)
            for g in range(c // sb):
                d8 = d8 + jnp.where(lane_blk == g, l_mat[g * sb:(g + 1) * sb, :], 0.0)
            packed.append(d8)
            w[ch, h] = dict(
                l=l_mat, a_intra=(wk["a"][c:] * decay).astype(BF16),
                rhs=jnp.concatenate([v * beta, wk["kb"] * eg], axis=1).astype(BF16),
                q_dec=wk["q"] * eg, k_dec_t=(wk["k"] * jnp.exp(g_last - gcc)).T.astype(BF16),
                d_chunk=jnp.exp(g_last))
        pall = jnp.concatenate(
            [jnp.concatenate(packed[2 * i:2 * i + 2], axis=1) for i in range(len(packed) // 2)], axis=0)
        w["packed"] = pall
        p_hi = pall.astype(BF16)
        p_lo = (pall - p_hi.astype(F32)).astype(BF16)
        spread = _dot(p_hi, bcat_ref[...]) + _dot(p_lo, bcat_ref[...])
        w["cols"] = [spread[:, j * LANES:(j + 1) * LANES] for j in range(sb - 1)]
        return w

    def diag_inverse(group, w):
        pall = w["packed"]
        n_pairs = pall.shape[0] // sb
        cols = w["cols"]
        tp = jnp.concatenate([eye_packed] * n_pairs, axis=0)
        for j in range(sb - 1):
            row_j = jnp.concatenate(
                [jnp.broadcast_to(tp[p * sb + j:p * sb + j + 1, :], (sb, LANES)) for p in range(n_pairs)], axis=0)
            tp = tp - cols[j] * row_j
        for i, key in enumerate([(ch, h) for ch in group for h in heads]):
            blocks = tp[(i // 2) * sb:(i // 2 + 1) * sb, (i % 2) * c:(i % 2 + 1) * c]
            w[key]["tinv"] = jnp.where(diag_blocks, jnp.tile(blocks, (c // sb, 1)), 0.0)
        return w

    def lower_rows(mat, width):
        return jnp.concatenate([mat[b * 2 * width + width:(b + 1) * 2 * width] for b in range(c // (2 * width))],
                               axis=0)

    def per_key(fn):
        def stage(group, w):
            for key in [(ch, h) for ch in group for h in heads]:
                fn(w[key])
            return w
        return stage

    def merge_first(mask, width):
        def fn(wk):
            wk["tb"] = wk["tinv"].astype(BF16)
            wk["half"] = _dot(lower_rows(wk["tinv"], width).astype(BF16),
                              jnp.where(mask, wk["l"], 0.0).astype(BF16)).astype(BF16)
        return per_key(fn)

    def merge_second(width):
        def fn(wk):
            t = wk["tinv"]
            new_low = lower_rows(t, width) - _dot(wk["half"], wk["tb"])
            parts = []
            for b in range(c // (2 * width)):
                parts += [t[b * 2 * width:b * 2 * width + width], new_low[b * width:(b + 1) * width]]
            wk["tinv"] = jnp.concatenate(parts, axis=0)
        return per_key(fn)

    def solve(wk):
        wk["uw"] = _dot(wk["tinv"].astype(BF16), wk["rhs"]).astype(BF16)

    def state_free(wk):
        both = _dot(jnp.concatenate([wk["a_intra"], wk["k_dec_t"]], axis=0), wk["uw"])
        wk["a_uw"] = both[:c]
        wk["k_uw"] = both[c:]
        wk["lhs"] = jnp.concatenate([wk["q_dec"] - both[:c, DN_DIM:], both[c:, DN_DIM:]],
                                    axis=0).astype(BF16)

    state = {(bi, h): state_ref[bi, h] for bi in range(nb) for h in heads}

    def recurrence(group, w):
        for ch in group:
            bi, r0 = locate(ch)
            prod = {h: _dot(w[ch, h]["lhs"], state[bi, h].astype(BF16)) for h in heads}
            for h in heads:
                wk = w[ch, h]
                o = prod[h][:c] + wk["a_uw"][:, :DN_DIM]
                state[bi, h] = state[bi, h] * wk["d_chunk"] - prod[h][c:] + wk["k_uw"][:, :DN_DIM]
                on = o * lax.rsqrt(jnp.mean(o * o, axis=-1, keepdims=True) + NORM_EPS) * gain_ref[...]
                zs = zs_ref[bi, r0:r0 + c, h * DN_DIM:(h + 1) * DN_DIM].astype(F32)
                o_ref[bi, r0:r0 + c, h * DN_DIM:(h + 1) * DN_DIM] = (on * zs).astype(BF16)

    stages = [load_and_gram, intra_chunk, diag_inverse]
    for level, mask in enumerate(merge_masks):
        stages += [merge_first(mask, sb << level), merge_second(sb << level)]
    stages += [per_key(solve), per_key(state_free), recurrence]
    n_items = nb * n_chunks
    group_size = min(DN_GROUP, n_items)
    groups = [list(range(g, g + group_size)) for g in range(0, n_items, group_size)]
    _staged_pipeline(groups, stages)
    for key, value in state.items():
        state_ref[key] = value


def _gdn_spread_matrix():
    sb = DN_SUB
    mat = np.zeros((LANES, (sb - 1) * LANES), np.float32)
    for j in range(sb - 1):
        for m in range(LANES // sb):
            mat[sb * m + j, j * LANES + sb * m:j * LANES + sb * (m + 1)] = 1.0
    return mat


def _gdn(dqkv, bg, zs, gain, tt):
    bsz, seq, _ = dqkv.shape
    nb = DN_SEQS if bsz % DN_SEQS == 0 else 1

    def tile(width):
        return pl.BlockSpec((nb, tt, width), lambda b, t: (b, t, 0))

    bcat = jnp.asarray(_gdn_spread_matrix(), BF16)
    return pl.pallas_call(
        _gdn_kernel,
        grid=(bsz // nb, seq // tt),
        in_specs=[tile(DN_CONV_DIM), tile(LANES), tile(DN_WIDTH), _const_spec(gain.shape),
                  _const_spec(bcat.shape)],
        out_specs=tile(DN_WIDTH),
        out_shape=jax.ShapeDtypeStruct((bsz, seq, DN_WIDTH), BF16),
        scratch_shapes=[pltpu.VMEM((nb, DN_HEADS, DN_DIM, DN_DIM), F32),
                        pltpu.VMEM((nb, tt, LANES), F32)],
        compiler_params=_params(2),
        name="gdn",
    )(dqkv, bg, zs, gain, bcat)


def _mix_ffn_kernel(x_ref, oa_ref, od_ref, gt_ref, mod_ref, g2_ref, wb_ref, wo_ref, wgu_ref, wd_ref, o_ref):
    tt = x_ref.shape[1]
    half_rows = tt // 2
    gate1 = mod_ref[0, 2:3, :]
    shift2 = mod_ref[0, 3:4, :]
    gain_scale2 = g2_ref[...] * (1.0 + mod_ref[0, 4:5, :])
    gate2 = mod_ref[0, 5:6, :]
    hid = [(0, FFN_SPLIT), (FFN_SPLIT, FFN_HIDDEN)]

    def rows(i):
        return slice(i * half_rows, (i + 1) * half_rows)

    def branch_proj(i, _):
        return dict(ya=_dot(oa_ref[0, rows(i), :], wb_ref[:ATTN_Q_WIDTH, :]),
                    yd=_dot(od_ref[0, rows(i), :], wb_ref[ATTN_Q_WIDTH:, :]))

    def merge(i, w):
        sa = _sigmoid_of_twice(gt_ref[0, rows(i), :D_MODEL].astype(F32))
        sd = _sigmoid_of_twice(gt_ref[0, rows(i), D_MODEL:].astype(F32))
        return dict(merged=(sa * w["ya"] + sd * w["yd"]).astype(BF16))

    def out_proj(i, w):
        return dict(out=_dot(w["merged"], wo_ref[...]))

    def residual_norm(i, w):
        x1 = x_ref[0, rows(i), :] + gate1 * w["out"]
        ms = jnp.mean(x1 * x1, axis=-1, keepdims=True)
        return dict(x1=x1, h=(x1 * lax.rsqrt(ms + NORM_EPS) * gain_scale2 + shift2).astype(BF16))

    def gate_up(part):
        lo, hi = hid[part]

        def stage(i, w):
            w[f"g{part}"] = _dot(w["h"], wgu_ref[:, lo:hi])
            w[f"u{part}"] = _dot(w["h"], wgu_ref[:, FFN_HIDDEN + lo:FFN_HIDDEN + hi])
            return w
        return stage

    def activation(part):
        def stage(i, w):
            w[f"act{part}"] = (_silu_of_twice(w.pop(f"g{part}")) * w.pop(f"u{part}")).astype(BF16)
            return w
        return stage

    def down_proj(i, w):
        w["y"] = sum(_dot(w[f"act{p}"], wd_ref[lo:hi, :]) for p, (lo, hi) in enumerate(hid))
        return w

    def finish(i, w):
        o_ref[0, rows(i), :] = w["x1"] + gate2 * w["y"]

    _staged_pipeline([0, 1], [branch_proj, merge, out_proj, residual_norm, gate_up(0), activation(0),
                              gate_up(1), activation(1), down_proj, finish])


def _mix_ffn(x, oa, od, gt, mod, g2, wb, wo, wgu, wd, tt):
    bsz, seq, d = x.shape

    def tile(width):
        return pl.BlockSpec((1, tt, width), lambda b, t: (b, t, 0))

    return pl.pallas_call(
        _mix_ffn_kernel,
        grid=(bsz, seq // tt),
        in_specs=[tile(d), tile(ATTN_Q_WIDTH), tile(DN_WIDTH), tile(2 * d),
                  pl.BlockSpec((1, 6, d), lambda b, t: (b, 0, 0)), _const_spec((1, d)),
                  _const_spec(wb.shape), _const_spec(wo.shape), _const_spec(wgu.shape), _const_spec(wd.shape)],
        out_specs=tile(d),
        out_shape=jax.ShapeDtypeStruct((bsz, seq, d), F32),
        compiler_params=_params(2),
        name="mix_ffn",
    )(x, oa, od, gt, mod, g2, wb, wo, wgu, wd)


def _rope_freqs():
    inv_freq = ROPE_THETA ** (-jnp.arange(0, ROT_DIM, 2, dtype=F32) / ROT_DIM)
    return jnp.broadcast_to(inv_freq[:, None], (ROT_DIM // 2, LANES))


def _pick_tile(seq, want):
    tt = min(want, seq)
    assert seq % tt == 0 and tt % ATTN_BLOCK == 0, (seq, tt)
    return tt


def kernel(x, c, positions, ada_w, ada_b, norm1_g, w_in, conv_w, q_norm_g, k_norm_g, sinks, a_log,
           dt_bias, dn_norm_g, w_branch, w_out, norm2_g, w_gate_up, w_down):
    bsz, seq, d = x.shape
    assert d == D_MODEL and ada_w.shape[0] == 1, "single-layer kernel"
    n_in = ATTN_Q_WIDTH + 2 * ATTN_KV_WIDTH + DN_CONV_DIM

    mod = _ada(c, ada_w[0], ada_b[0]).reshape(bsz, 6, d)

    w = w_in[0]
    w_packed = jnp.concatenate(
        [w[:, :n_in], w[:, n_in:n_in + 2 * DN_HEADS],
         jnp.zeros((d, LANES - 2 * DN_HEADS), w.dtype), 0.5 * w[:, n_in + 2 * DN_HEADS:]],
        axis=1).astype(BF16)
    assert w_packed.shape[1] == IN_PACKED
    wgu = jnp.concatenate([0.5 * w_gate_up[0][:, :FFN_HIDDEN], w_gate_up[0][:, FFN_HIDDEN:]],
                          axis=1).astype(BF16)

    head_id = np.arange(ATTN_Q_WIDTH) // ATTN_HEAD_DIM
    bd = jnp.asarray(head_id[:, None] == head_id[None, :], BF16)
    qkg = jnp.stack([jnp.tile(q_norm_g[0], ATTN_HEADS),
                     jnp.pad(jnp.tile(k_norm_g[0], ATTN_KV_HEADS), (0, ATTN_Q_WIDTH - ATTN_KV_WIDTH))])
    freq = _rope_freqs()
    convw = 0.5 * conv_w[0].reshape(DN_CONV, DN_CONV_DIM)
    pad4 = (DN_HEADS, LANES - 2 * DN_HEADS)
    alog = jnp.pad(a_log[0], pad4).reshape(1, LANES)
    dtb = jnp.pad(dt_bias[0], pad4).reshape(1, LANES)
    sink_lanes = jnp.repeat(sinks[0], ATTN_HEAD_DIM).reshape(ATTN_HEADS // 2, LANES)

    q, k, v, dqkv, bg, zs, sg = _inproj(
        x, mod, positions.reshape(bsz, 1, seq), norm1_g[0].reshape(1, d), w_packed, bd, qkg, freq,
        convw, alog, dtb, _pick_tile(seq, 256))
    oa = _attn(q, k, v, sink_lanes, _pick_tile(seq, 512))
    od = _gdn(dqkv, bg, zs, dn_norm_g[0].reshape(1, DN_DIM), _pick_tile(seq, 512))
    return _mix_ffn(x, oa, od, sg, mod, norm2_g[0].reshape(1, d), w_branch[0].astype(BF16),
                    w_out[0].astype(BF16), wgu, w_down[0].astype(BF16), _pick_tile(seq, 512))
```

```python
import functools

import jax
import jax.numpy as jnp
import numpy as np
from jax import lax
from jax.experimental import pallas as pl
from jax.experimental.pallas import tpu as pltpu

F32 = jnp.float32
BF16 = jnp.bfloat16

D_MODEL = 1024
ATTN_HEADS = 8
ATTN_KV_HEADS = 2
ATTN_HEAD_DIM = 64
ATTN_BLOCK = 128
ROT_DIM = ATTN_HEAD_DIM // 4
ROPE_THETA = 500000.0
ATTN_Q_WIDTH = ATTN_HEADS * ATTN_HEAD_DIM
ATTN_KV_WIDTH = ATTN_KV_HEADS * ATTN_HEAD_DIM
DN_HEADS = 4
DN_DIM = 128
DN_CONV = 4
DN_CHUNK = 64
DN_SUB = 8
DN_GROUP = 2
DN_SEQS = 2
DN_WIDTH = DN_HEADS * DN_DIM
DN_CONV_DIM = 3 * DN_WIDTH
FFN_HIDDEN = 2816
FFN_SPLIT = 1536
NORM_EPS = 1e-6
LANES = 128

COL_QKV = 0
COL_DN = COL_QKV + ATTN_Q_WIDTH + 2 * ATTN_KV_WIDTH
COL_BA = COL_DN + DN_CONV_DIM
COL_Z = COL_BA + LANES
COL_GATES = COL_Z + DN_WIDTH
IN_PACKED = COL_GATES + 2 * D_MODEL

VMEM_LIMIT = 56 * 1024 * 1024


def _sigmoid(x):
    return 0.5 * jnp.tanh(0.5 * x) + 0.5


def _sigmoid_of_twice(xh):
    return 0.5 * jnp.tanh(xh) + 0.5


def _silu_of_twice(xh):
    return xh * jnp.tanh(xh) + xh


def _dot(a, b):
    return jnp.dot(a, b, preferred_element_type=F32)


def _dot_nt(a, b):
    return lax.dot_general(a, b, (((1,), (1,)), ((), ())), preferred_element_type=F32)


def _software_pipeline(stages):
    pending = None
    for issue, finish in stages:
        value = issue()
        if pending is not None:
            pending[0](pending[1])
        pending = (finish, value)
    pending[0](pending[1])


def _staged_pipeline(items, stages):
    depth = len(stages)
    values = {}
    for step in range(len(items) + depth - 1):
        for k in range(depth):
            i = step - k
            if 0 <= i < len(items):
                values[i] = stages[k](items[i], values.get(i))
                if k == depth - 1:
                    del values[i]


def _const_spec(shape):
    nd = len(shape)
    return pl.BlockSpec(shape, lambda *_: (0,) * nd, pipeline_mode=pl.Buffered(1))


def _params(n_grid):
    return pltpu.CompilerParams(dimension_semantics=("arbitrary",) * n_grid,
                                vmem_limit_bytes=VMEM_LIMIT)


def _ada_kernel(c_ref, w_ref, b_ref, o_ref):
    c = c_ref[...]
    cond = c * _sigmoid(c)
    o_ref[...] = jnp.dot(cond, w_ref[...], preferred_element_type=F32,
                         precision=lax.Precision.HIGHEST) + b_ref[...]


def _ada(c, w, b):
    bsz, d = c.shape
    n = w.shape[1]
    bn = 1536
    return pl.pallas_call(
        _ada_kernel,
        grid=(n // bn,),
        in_specs=[pl.BlockSpec((bsz, d), lambda j: (0, 0)),
                  pl.BlockSpec((d, bn), lambda j: (0, j)),
                  pl.BlockSpec((1, bn), lambda j: (0, j))],
        out_specs=pl.BlockSpec((bsz, bn), lambda j: (0, j)),
        out_shape=jax.ShapeDtypeStruct((bsz, n), F32),
        compiler_params=_params(1),
        name="ada",
    )(c, w, b.reshape(1, n))


def _inproj_kernel(tiles_per_seq, x_ref, mod_ref, pos_ref, g1_ref, w_ref, bd_ref, qkg_ref, freq_ref,
                   convw_ref, alog_ref, dtb_ref,
                   q_ref, k_ref, v_ref, dqkv_ref, bg_ref, zs_ref, sg_ref,
                   xc_ref, h_ref):
    i = pl.program_id(0)
    tt = x_ref.shape[1]
    t = jnp.maximum(i - 1, 0) % tiles_per_seq
    slot_cur = (i + 1) % 2
    slot_new = i % 2

    @pl.when(i == 0)
    def _():
        h_ref[1] = jnp.zeros(h_ref.shape[1:], BF16)

    h = h_ref[slot_cur]

    def normalise_next_tile():
        x = x_ref[0]
        ms = jnp.mean(x * x, axis=-1, keepdims=True)
        gain_scale = g1_ref[...] * (1.0 + mod_ref[0, 1:2, :])
        h_ref[slot_new] = (x * lax.rsqrt(ms + NORM_EPS) * gain_scale + mod_ref[0, 0:1, :]).astype(BF16)

    @pl.when(t == 0)
    def _():
        xc_ref[0:8, :] = jnp.zeros((8, DN_CONV_DIM), F32)

    ang = jnp.tile(freq_ref[...], (1, tt // LANES)) * pos_ref[0].astype(F32)
    cos_t = jnp.cos(ang)
    sin_t = jnp.sin(ang)
    one_t = jnp.ones_like(cos_t)
    zero_t = jnp.zeros_like(cos_t)

    def lane_table(first, second, fill):
        head = [first, second] + [fill] * (ATTN_HEAD_DIM // 8 - 2)
        return jnp.concatenate(head * (LANES // ATTN_HEAD_DIM), axis=0).T

    cos = lane_table(cos_t, cos_t, one_t)
    sin_signed = lane_table(-sin_t, sin_t, zero_t)
    half = ROT_DIM // 2
    first_half = (lax.broadcasted_iota(jnp.int32, (tt, LANES), 1) % ROT_DIM) < half

    def project(lo, hi):
        return lambda: _dot(h, w_ref[:, lo:hi])

    def finish_qkv(qkv):
        q = qkv[:, :ATTN_Q_WIDTH]
        k = qkv[:, ATTN_Q_WIDTH:ATTN_Q_WIDTH + ATTN_KV_WIDTH]
        v = qkv[:, ATTN_Q_WIDTH + ATTN_KV_WIDTH:]
        bd = bd_ref[...]
        ssq_q = _dot((q * q).astype(BF16), bd)
        ssq_k = _dot((k * k).astype(BF16), bd[:ATTN_KV_WIDTH, :ATTN_KV_WIDTH])
        qn = q * lax.rsqrt(ssq_q * (1.0 / ATTN_HEAD_DIM) + NORM_EPS) * qkg_ref[0:1, :]
        kn = k * lax.rsqrt(ssq_k * (1.0 / ATTN_HEAD_DIM) + NORM_EPS) * qkg_ref[1:2, :ATTN_KV_WIDTH]

        def rope(u):
            partner = jnp.where(first_half, pltpu.roll(u, LANES - half, axis=1), pltpu.roll(u, half, axis=1))
            return u * cos + partner * sin_signed

        for i in range(ATTN_Q_WIDTH // LANES):
            sl = slice(i * LANES, (i + 1) * LANES)
            q_ref[0, :, sl] = (rope(qn[:, sl]) * (ATTN_HEAD_DIM ** -0.5)).astype(BF16)
        k_ref[0] = rope(kn).astype(BF16)
        v_ref[0] = v.astype(BF16)

    def finish_conv(group):
        c0 = group * DN_WIDTH

        def finish(proj):
            cols = slice(c0, c0 + DN_WIDTH)
            carry = xc_ref[:, cols]
            xc_ref[:, cols] = proj[tt - 8:, :]
            row8 = lax.broadcasted_iota(jnp.int32, (8, DN_WIDTH), 0)
            conv = convw_ref[DN_CONV - 1:DN_CONV, cols] * proj
            slabs = [carry] + [proj[8 * g:8 * (g + 1), :] for g in range(tt // 8)]
            for s in range(1, DN_CONV):
                rot = [pltpu.roll(slab, s, axis=0) for slab in slabs]
                shifted = jnp.concatenate(
                    [jnp.where(row8 < s, rot[g], rot[g + 1]) for g in range(tt // 8)], axis=0)
                conv = conv + convw_ref[DN_CONV - 1 - s:DN_CONV - s, cols] * shifted
            act = _silu_of_twice(conv)
            if group == 2:
                dqkv_ref[0, :, cols] = act.astype(BF16)
                return
            scale = DN_DIM ** -0.5 if group == 0 else 1.0
            for i in range(DN_HEADS):
                u = act[:, i * DN_DIM:(i + 1) * DN_DIM]
                un = u * (lax.rsqrt(jnp.sum(u * u, axis=-1, keepdims=True) + NORM_EPS) * scale)
                dqkv_ref[0, :, c0 + i * DN_DIM:c0 + (i + 1) * DN_DIM] = un.astype(BF16)

        return finish

    def finish_ba_z(baz):
        ba = baz[:, :LANES]
        z = baz[:, LANES:]
        sp_in = ba + dtb_ref[...]
        softplus = jnp.maximum(sp_in, 0.0) + jnp.log(1.0 + jnp.exp(-jnp.abs(sp_in)))
        g = -jnp.exp(alog_ref[...]) * softplus
        lane = lax.broadcasted_iota(jnp.int32, ba.shape, 1)
        bg_ref[0] = jnp.where(lane < DN_HEADS, _sigmoid(ba), g)
        zs_ref[0] = _silu_of_twice(z).astype(BF16)

    def finish_gate(i):
        def finish(gate):
            sg_ref[0, :, i * D_MODEL:(i + 1) * D_MODEL] = gate.astype(BF16)
            if i == 0:
                normalise_next_tile()
        return finish

    conv = [(project(COL_DN + g * DN_WIDTH, COL_DN + (g + 1) * DN_WIDTH), finish_conv(g)) for g in range(3)]
    gate = [(project(COL_GATES + i * D_MODEL, COL_GATES + (i + 1) * D_MODEL), finish_gate(i)) for i in range(2)]
    _software_pipeline([(project(COL_QKV, COL_DN), finish_qkv), conv[0], gate[0], conv[1], gate[1], conv[2],
                        (project(COL_BA, COL_GATES), finish_ba_z)])


def _inproj(x, mod, pos, g1, w_packed, bd, qkg, freq, convw, alog, dtb, tt):
    bsz, seq, d = x.shape
    tps = seq // tt
    n = bsz * tps

    def cur(i):
        return jnp.minimum(i, n - 1)

    def prev(i):
        return jnp.maximum(i - 1, 0)

    def tile(width):
        return pl.BlockSpec((1, tt, width), lambda i: (prev(i) // tps, prev(i) % tps, 0))

    out_shapes = [
        jax.ShapeDtypeStruct((bsz, seq, ATTN_Q_WIDTH), BF16),
        jax.ShapeDtypeStruct((bsz, seq, ATTN_KV_WIDTH), BF16),
        jax.ShapeDtypeStruct((bsz, seq, ATTN_KV_WIDTH), BF16),
        jax.ShapeDtypeStruct((bsz, seq, DN_CONV_DIM), BF16),
        jax.ShapeDtypeStruct((bsz, seq, LANES), F32),
        jax.ShapeDtypeStruct((bsz, seq, DN_WIDTH), BF16),
        jax.ShapeDtypeStruct((bsz, seq, 2 * D_MODEL), BF16),
    ]
    return pl.pallas_call(
        functools.partial(_inproj_kernel, tps),
        grid=(n + 1,),
        in_specs=[pl.BlockSpec((1, tt, d), lambda i: (cur(i) // tps, cur(i) % tps, 0)),
                  pl.BlockSpec((1, 6, d), lambda i: (cur(i) // tps, 0, 0)),
                  pl.BlockSpec((1, 1, tt), lambda i: (prev(i) // tps, 0, prev(i) % tps)),
                  _const_spec((1, d)),
                  _const_spec(w_packed.shape),
                  _const_spec(bd.shape),
                  _const_spec(qkg.shape),
                  _const_spec(freq.shape),
                  _const_spec(convw.shape),
                  _const_spec(alog.shape),
                  _const_spec(dtb.shape)],
        out_specs=[tile(ATTN_Q_WIDTH), tile(ATTN_KV_WIDTH), tile(ATTN_KV_WIDTH), tile(DN_CONV_DIM),
                   tile(LANES), tile(DN_WIDTH), tile(2 * D_MODEL)],
        out_shape=out_shapes,
        scratch_shapes=[pltpu.VMEM((8, DN_CONV_DIM), F32),
                        pltpu.VMEM((2, tt, d), BF16)],
        compiler_params=_params(1),
        name="inproj",
    )(x, mod, pos, g1, w_packed, bd, qkg, freq, convw, alog, dtb)


def _attn_kernel(q_ref, k_ref, v_ref, kp_ref, vp_ref, sink_ref, o_ref, kw_ref, vw_ref):
    t = pl.program_id(1)
    tt = q_ref.shape[1]
    blk = ATTN_BLOCK
    hd = ATTN_HEAD_DIM

    kw_ref[blk:, :] = k_ref[0]
    vw_ref[blk:, :] = v_ref[0]

    @pl.when(t == 0)
    def _():
        kw_ref[:blk, :] = jnp.zeros((blk, ATTN_KV_WIDTH), BF16)
        vw_ref[:blk, :] = jnp.zeros((blk, ATTN_KV_WIDTH), BF16)

    @pl.when(t > 0)
    def _():
        kw_ref[:blk, :] = kp_ref[0]
        vw_ref[:blk, :] = vp_ref[0]

    row = lax.broadcasted_iota(jnp.int32, (blk, 2 * blk), 0)
    col = lax.broadcasted_iota(jnp.int32, (blk, 2 * blk), 1)
    band = (col > row) & (col <= row + blk)
    win = tt + blk
    lo_kv = lax.broadcasted_iota(jnp.int32, (win, LANES), 1) < hd
    lo_q = lax.broadcasted_iota(jnp.int32, (blk, LANES), 1) < hd
    ones_lo = jnp.where(lo_kv, 1.0, 0.0).astype(BF16)
    ones_hi = jnp.where(lo_kv, 0.0, 1.0).astype(BF16)

    kwin = kw_ref[...].astype(F32)
    vwin = vw_ref[...].astype(F32)
    kswp = pltpu.roll(kwin, hd, axis=1)
    vswp = pltpu.roll(vwin, hd, axis=1)
    k_lo, k_hi, v_lo, v_hi = [], [], [], []
    for g in range(ATTN_KV_HEADS):
        k_lo.append(jnp.where(lo_kv, kwin if g == 0 else kswp, 0.0).astype(BF16))
        k_hi.append(jnp.where(lo_kv, 0.0, kswp if g == 0 else kwin).astype(BF16))
        v_lo.append(jnp.concatenate(
            [jnp.where(lo_kv, vwin if g == 0 else vswp, 0.0).astype(BF16), ones_lo], axis=1))
        v_hi.append(jnp.concatenate(
            [jnp.where(lo_kv, 0.0, vswp if g == 0 else vwin).astype(BF16), ones_hi], axis=1))

    valid_first = band & (col >= jnp.where(t == 0, blk, 0))
    items = [(j, p) for j in range(tt // blk) for p in range(ATTN_HEADS // 2)]

    def scores(item, _):
        j, p = item
        rows = slice(j * blk, (j + 2) * blk)
        return _dot_nt(q_ref[0, j * blk:(j + 1) * blk, p * LANES:(p + 1) * LANES],
                       jnp.concatenate([k_lo[p // 2][rows], k_hi[p // 2][rows]], axis=0))

    def softmax(item, s):
        j, p = item
        valid = valid_first if j == 0 else band
        ps, ms = [], []
        for e in range(2):
            se = jnp.where(valid, s[:, e * 2 * blk:(e + 1) * 2 * blk], -jnp.inf)
            m = jnp.maximum(jnp.max(se, axis=-1, keepdims=True), sink_ref[p:p + 1, e * hd:e * hd + 1])
            ps.append(jnp.exp(se - m).astype(BF16))
            ms.append(m)
        return jnp.concatenate(ps, axis=1), jnp.where(lo_q, ms[0], ms[1])

    def values(item, soft):
        j, p = item
        rows = slice(j * blk, (j + 2) * blk)
        pv = _dot(soft[0], jnp.concatenate([v_lo[p // 2][rows], v_hi[p // 2][rows]], axis=0))
        return pv, soft[1]

    def finish(item, res):
        j, p = item
        pv, m_pair = res
        denom = pv[:, LANES:] + jnp.exp(sink_ref[p:p + 1, :] - m_pair)
        o_ref[0, j * blk:(j + 1) * blk, p * LANES:(p + 1) * LANES] = (pv[:, :LANES] / denom).astype(BF16)

    _staged_pipeline(items, [scores, softmax, values, finish])


def _attn(q, k, v, sink_lanes, tt):
    bsz, seq, _ = q.shape
    nprev = tt // ATTN_BLOCK

    def tile(width):
        return pl.BlockSpec((1, tt, width), lambda b, t: (b, t, 0))

    def prev(width):
        return pl.BlockSpec((1, ATTN_BLOCK, width),
                            lambda b, t: (b, jnp.maximum(t * nprev - 1, 0), 0))

    return pl.pallas_call(
        _attn_kernel,
        grid=(bsz, seq // tt),
        in_specs=[tile(ATTN_Q_WIDTH), tile(ATTN_KV_WIDTH), tile(ATTN_KV_WIDTH),
                  prev(ATTN_KV_WIDTH), prev(ATTN_KV_WIDTH),
                  _const_spec(sink_lanes.shape)],
        out_specs=tile(ATTN_Q_WIDTH),
        out_shape=jax.ShapeDtypeStruct((bsz, seq, ATTN_Q_WIDTH), BF16),
        scratch_shapes=[pltpu.VMEM((tt + ATTN_BLOCK, ATTN_KV_WIDTH), BF16),
                        pltpu.VMEM((tt + ATTN_BLOCK, ATTN_KV_WIDTH), BF16)],
        compiler_params=_params(2),
        name="attn",
    )(q, k, v, k, v, sink_lanes)


def _gdn_kernel(dqkv_ref, bg_ref, zs_ref, gain_ref, bcat_ref, o_ref, state_ref, gc_ref):
    t = pl.program_id(1)
    nb, tt = dqkv_ref.shape[0], dqkv_ref.shape[1]
    c = DN_CHUNK
    n_chunks = tt // c

    @pl.when(t == 0)
    def _():
        state_ref[...] = jnp.zeros(state_ref.shape, F32)

    rin = lax.broadcasted_iota(jnp.int32, (tt, LANES), 0) % c
    gcts = []
    for bi in range(nb):
        gc = bg_ref[bi]
        step = 1
        while step < c:
            gc = gc + jnp.where(rin >= step, pltpu.roll(gc, step, axis=0), 0.0)
            step *= 2
        gc_ref[bi] = gc
        gcts.append(gc.T)

    def locate(ch):
        return ch % nb, (ch // nb) * c

    ri = lax.broadcasted_iota(jnp.int32, (c, c), 0)
    ci = lax.broadcasted_iota(jnp.int32, (c, c), 1)
    incl = ri >= ci
    strict = ri > ci
    sb = DN_SUB
    diag_blocks = (ri // sb) == (ci // sb)
    merge_masks = []
    width = sb
    while width < c:
        merge_masks.append(((ri // (2 * width)) == (ci // (2 * width))) & ((ri // width) != (ci // width)))
        width *= 2
    lane_blk = lax.broadcasted_iota(jnp.int32, (sb, c), 1) // sb
    sub_p = lax.broadcasted_iota(jnp.int32, (sb, LANES), 0)
    lane_p = lax.broadcasted_iota(jnp.int32, (sb, LANES), 1)
    eye_packed = jnp.where(sub_p == lane_p % sb, 1.0, 0.0).astype(F32)

    heads = range(DN_HEADS)

    def load_and_gram(group, _):
        w = {}
        for ch, h in group:
            bi, r0 = locate(ch)
            q = dqkv_ref[bi, r0:r0 + c, h * DN_DIM:(h + 1) * DN_DIM].astype(F32)
            k = dqkv_ref[bi, r0:r0 + c, DN_WIDTH + h * DN_DIM:DN_WIDTH + (h + 1) * DN_DIM].astype(F32)
            kb = k * bg_ref[bi, r0:r0 + c, h:h + 1]
            a = _dot_nt(jnp.concatenate([kb, q], axis=0).astype(BF16), k.astype(BF16))
            w[ch, h] = dict(q=q, k=k, kb=kb, a=a)
        return w

    def intra_chunk(group, w):
        packed = []
        for ch, h in group:
            bi, r0 = locate(ch)
            wk = w[ch, h]
            v = dqkv_ref[bi, r0:r0 + c, 2 * DN_WIDTH + h * DN_DIM:2 * DN_WIDTH + (h + 1) * DN_DIM].astype(F32)
            beta = bg_ref[bi, r0:r0 + c, h:h + 1]
            gcc = gc_ref[bi, r0:r0 + c, DN_HEADS + h:DN_HEADS + h + 1]
            gcr = gcts[bi][DN_HEADS + h:DN_HEADS + h + 1, r0:r0 + c]
            g_last = gcc[c - 1:c, :]
            eg = jnp.exp(gcc)
            decay = jnp.exp(jnp.where(incl, gcc - gcr, -jnp.inf))
            l_mat = jnp.where(strict, wk["a"][:c] * decay, 0.0)
            d8 = jnp.zeros((sb, c), F32)
            for g in range(c // sb):
                d8 = d8 + jnp.where(lane_blk == g, l_mat[g * sb:(g + 1) * sb, :], 0.0)
            packed.append(d8)
            w[ch, h] = dict(
                l=l_mat, a_intra=(wk["a"][c:] * decay).astype(BF16),
                rhs=jnp.concatenate([v * beta, wk["kb"] * eg], axis=1).astype(BF16),
                q_dec=wk["q"] * eg, k_dec_t=(wk["k"] * jnp.exp(g_last - gcc)).T.astype(BF16),
                d_chunk=jnp.exp(g_last))
        pall = jnp.concatenate(
            [jnp.concatenate(packed[2 * i:2 * i + 2], axis=1) for i in range(len(packed) // 2)], axis=0)
        w["packed"] = pall
        p_hi = pall.astype(BF16)
        p_lo = (pall - p_hi.astype(F32)).astype(BF16)
        spread = _dot(p_hi, bcat_ref[...]) + _dot(p_lo, bcat_ref[...])
        w["cols"] = [spread[:, j * LANES:(j + 1) * LANES] for j in range(sb - 1)]
        return w

    def diag_inverse(group, w):
        pall = w["packed"]
        n_pairs = pall.shape[0] // sb
        cols = w["cols"]
        tp = jnp.concatenate([eye_packed] * n_pairs, axis=0)
        for j in range(sb - 1):
            row_j = jnp.concatenate(
                [jnp.broadcast_to(tp[p * sb + j:p * sb + j + 1, :], (sb, LANES)) for p in range(n_pairs)], axis=0)
            tp = tp - cols[j] * row_j
        for i, key in enumerate(group):
            blocks = tp[(i // 2) * sb:(i // 2 + 1) * sb, (i % 2) * c:(i % 2 + 1) * c]
            w[key]["tinv"] = jnp.where(diag_blocks, jnp.tile(blocks, (c // sb, 1)), 0.0)
        return w

    def lower_rows(mat, width):
        return jnp.concatenate([mat[b * 2 * width + width:(b + 1) * 2 * width] for b in range(c // (2 * width))],
                               axis=0)

    def per_key(fn):
        def stage(group, w):
            for key in group:
                fn(w[key])
            return w
        return stage

    def merge_first(mask, width):
        def fn(wk):
            wk["tb"] = wk["tinv"].astype(BF16)
            wk["half"] = _dot(lower_rows(wk["tinv"], width).astype(BF16),
                              jnp.where(mask, wk["l"], 0.0).astype(BF16)).astype(BF16)
        return per_key(fn)

    def merge_second(width):
        def fn(wk):
            t = wk["tinv"]
            new_low = lower_rows(t, width) - _dot(wk["half"], wk["tb"])
            parts = []
            for b in range(c // (2 * width)):
                parts += [t[b * 2 * width:b * 2 * width + width], new_low[b * width:(b + 1) * width]]
            wk["tinv"] = jnp.concatenate(parts, axis=0)
        return per_key(fn)

    def solve(wk):
        wk["uw"] = _dot(wk["tinv"].astype(BF16), wk["rhs"]).astype(BF16)

    def state_free(wk):
        both = _dot(jnp.concatenate([wk["a_intra"], wk["k_dec_t"]], axis=0), wk["uw"])
        wk["a_uw"] = both[:c]
        wk["k_uw"] = both[c:]
        wk["lhs"] = jnp.concatenate([wk["q_dec"] - both[:c, DN_DIM:], both[c:, DN_DIM:]],
                                    axis=0).astype(BF16)

    state = {(bi, h): state_ref[bi, h] for bi in range(nb) for h in heads}

    def recurrence(group, w):
        prod = {}
        for ch, h in group:
            prod[ch, h] = _dot(w[ch, h]["lhs"], state[locate(ch)[0], h].astype(BF16))
        for ch, h in group:
            bi, r0 = locate(ch)
            wk = w[ch, h]
            o = prod[ch, h][:c] + wk["a_uw"][:, :DN_DIM]
            state[bi, h] = state[bi, h] * wk["d_chunk"] - prod[ch, h][c:] + wk["k_uw"][:, :DN_DIM]
            on = o * lax.rsqrt(jnp.mean(o * o, axis=-1, keepdims=True) + NORM_EPS) * gain_ref[...]
            zs = zs_ref[bi, r0:r0 + c, h * DN_DIM:(h + 1) * DN_DIM].astype(F32)
            o_ref[bi, r0:r0 + c, h * DN_DIM:(h + 1) * DN_DIM] = (on * zs).astype(BF16)

    stages = [load_and_gram, intra_chunk, diag_inverse]
    for level, mask in enumerate(merge_masks):
        stages += [merge_first(mask, sb << level), merge_second(sb << level)]
    stages += [per_key(solve), per_key(state_free), recurrence]
    n_slots = nb * n_chunks
    size = min(DN_GROUP, n_slots)
    groups = [[(ch, h) for ch in range(g, g + size) for h in heads] for g in range(0, n_slots, size)]
    _staged_pipeline(groups, stages)
    for key, value in state.items():
        state_ref[key] = value


def _gdn_spread_matrix():
    sb = DN_SUB
    mat = np.zeros((LANES, (sb - 1) * LANES), np.float32)
    for j in range(sb - 1):
        for m in range(LANES // sb):
            mat[sb * m + j, j * LANES + sb * m:j * LANES + sb * (m + 1)] = 1.0
    return mat


def _gdn(dqkv, bg, zs, gain, tt):
    bsz, seq, _ = dqkv.shape
    nb = DN_SEQS if bsz % DN_SEQS == 0 else 1

    def tile(width):
        return pl.BlockSpec((nb, tt, width), lambda b, t: (b, t, 0))

    bcat = jnp.asarray(_gdn_spread_matrix(), BF16)
    return pl.pallas_call(
        _gdn_kernel,
        grid=(bsz // nb, seq // tt),
        in_specs=[tile(DN_CONV_DIM), tile(LANES), tile(DN_WIDTH), _const_spec(gain.shape),
                  _const_spec(bcat.shape)],
        out_specs=tile(DN_WIDTH),
        out_shape=jax.ShapeDtypeStruct((bsz, seq, DN_WIDTH), BF16),
        scratch_shapes=[pltpu.VMEM((nb, DN_HEADS, DN_DIM, DN_DIM), F32),
                        pltpu.VMEM((nb, tt, LANES), F32)],
        compiler_params=_params(2),
        name="gdn",
    )(dqkv, bg, zs, gain, bcat)


def _mix_ffn_kernel(x_ref, oa_ref, od_ref, gt_ref, mod_ref, g2_ref, wb_ref, wo_ref, wgu_ref, wd_ref, o_ref):
    tt = x_ref.shape[1]
    half_rows = tt // 2
    gate1 = mod_ref[0, 2:3, :]
    shift2 = mod_ref[0, 3:4, :]
    gain_scale2 = g2_ref[...] * (1.0 + mod_ref[0, 4:5, :])
    gate2 = mod_ref[0, 5:6, :]
    hid = [(0, FFN_SPLIT), (FFN_SPLIT, FFN_HIDDEN)]

    def rows(i):
        return slice(i * half_rows, (i + 1) * half_rows)

    def branch_proj(i, _):
        return dict(ya=_dot(oa_ref[0, rows(i), :], wb_ref[:ATTN_Q_WIDTH, :]),
                    yd=_dot(od_ref[0, rows(i), :], wb_ref[ATTN_Q_WIDTH:, :]))

    def merge(i, w):
        sa = _sigmoid_of_twice(gt_ref[0, rows(i), :D_MODEL].astype(F32))
        sd = _sigmoid_of_twice(gt_ref[0, rows(i), D_MODEL:].astype(F32))
        return dict(merged=(sa * w["ya"] + sd * w["yd"]).astype(BF16))

    def out_proj(i, w):
        return dict(out=_dot(w["merged"], wo_ref[...]))

    def residual_norm(i, w):
        x1 = x_ref[0, rows(i), :] + gate1 * w["out"]
        ms = jnp.mean(x1 * x1, axis=-1, keepdims=True)
        return dict(x1=x1, h=(x1 * lax.rsqrt(ms + NORM_EPS) * gain_scale2 + shift2).astype(BF16))

    def gate_up(part):
        lo, hi = hid[part]

        def stage(i, w):
            w[f"g{part}"] = _dot(w["h"], wgu_ref[:, lo:hi])
            w[f"u{part}"] = _dot(w["h"], wgu_ref[:, FFN_HIDDEN + lo:FFN_HIDDEN + hi])
            return w
        return stage

    def activation(part):
        def stage(i, w):
            w[f"act{part}"] = (_silu_of_twice(w.pop(f"g{part}")) * w.pop(f"u{part}")).astype(BF16)
            return w
        return stage

    def down_proj(i, w):
        w["y"] = sum(_dot(w[f"act{p}"], wd_ref[lo:hi, :]) for p, (lo, hi) in enumerate(hid))
        return w

    def finish(i, w):
        o_ref[0, rows(i), :] = w["x1"] + gate2 * w["y"]

    _staged_pipeline([0, 1], [branch_proj, merge, out_proj, residual_norm, gate_up(0), activation(0),
                              gate_up(1), activation(1), down_proj, finish])


def _mix_ffn(x, oa, od, gt, mod, g2, wb, wo, wgu, wd, tt):
    bsz, seq, d = x.shape

    def tile(width):
        return pl.BlockSpec((1, tt, width), lambda b, t: (b, t, 0))

    return pl.pallas_call(
        _mix_ffn_kernel,
        grid=(bsz, seq // tt),
        in_specs=[tile(d), tile(ATTN_Q_WIDTH), tile(DN_WIDTH), tile(2 * d),
                  pl.BlockSpec((1, 6, d), lambda b, t: (b, 0, 0)), _const_spec((1, d)),
                  _const_spec(wb.shape), _const_spec(wo.shape), _const_spec(wgu.shape), _const_spec(wd.shape)],
        out_specs=tile(d),
        out_shape=jax.ShapeDtypeStruct((bsz, seq, d), F32),
        compiler_params=_params(2),
        name="mix_ffn",
    )(x, oa, od, gt, mod, g2, wb, wo, wgu, wd)


def _rope_freqs():
    inv_freq = ROPE_THETA ** (-jnp.arange(0, ROT_DIM, 2, dtype=F32) / ROT_DIM)
    return jnp.broadcast_to(inv_freq[:, None], (ROT_DIM // 2, LANES))


def _pick_tile(seq, want):
    tt = min(want, seq)
    assert seq % tt == 0 and tt % ATTN_BLOCK == 0, (seq, tt)
    return tt


def kernel(x, c, positions, ada_w, ada_b, norm1_g, w_in, conv_w, q_norm_g, k_norm_g, sinks, a_log,
           dt_bias, dn_norm_g, w_branch, w_out, norm2_g, w_gate_up, w_down):
    bsz, seq, d = x.shape
    assert d == D_MODEL and ada_w.shape[0] == 1, "single-layer kernel"
    n_in = ATTN_Q_WIDTH + 2 * ATTN_KV_WIDTH + DN_CONV_DIM

    mod = _ada(c, ada_w[0], ada_b[0]).reshape(bsz, 6, d)

    w = w_in[0]
    w_packed = jnp.concatenate(
        [w[:, :n_in], w[:, n_in:n_in + 2 * DN_HEADS],
         jnp.zeros((d, LANES - 2 * DN_HEADS), w.dtype), 0.5 * w[:, n_in + 2 * DN_HEADS:]],
        axis=1).astype(BF16)
    assert w_packed.shape[1] == IN_PACKED
    wgu = jnp.concatenate([0.5 * w_gate_up[0][:, :FFN_HIDDEN], w_gate_up[0][:, FFN_HIDDEN:]],
                          axis=1).astype(BF16)

    head_id = np.arange(ATTN_Q_WIDTH) // ATTN_HEAD_DIM
    bd = jnp.asarray(head_id[:, None] == head_id[None, :], BF16)
    qkg = jnp.stack([jnp.tile(q_norm_g[0], ATTN_HEADS),
                     jnp.pad(jnp.tile(k_norm_g[0], ATTN_KV_HEADS), (0, ATTN_Q_WIDTH - ATTN_KV_WIDTH))])
    freq = _rope_freqs()
    convw = 0.5 * conv_w[0].reshape(DN_CONV, DN_CONV_DIM)
    pad4 = (DN_HEADS, LANES - 2 * DN_HEADS)
    alog = jnp.pad(a_log[0], pad4).reshape(1, LANES)
    dtb = jnp.pad(dt_bias[0], pad4).reshape(1, LANES)
    sink_lanes = jnp.repeat(sinks[0], ATTN_HEAD_DIM).reshape(ATTN_HEADS // 2, LANES)

    q, k, v, dqkv, bg, zs, sg = _inproj(
        x, mod, positions.reshape(bsz, 1, seq), norm1_g[0].reshape(1, d), w_packed, bd, qkg, freq,
        convw, alog, dtb, _pick_tile(seq, 256))
    oa = _attn(q, k, v, sink_lanes, _pick_tile(seq, 2048))
    od = _gdn(dqkv, bg, zs, dn_norm_g[0].reshape(1, DN_DIM), _pick_tile(seq, 512))
    return _mix_ffn(x, oa, od, sg, mod, norm2_g[0].reshape(1, d), w_branch[0].astype(BF16),
                    w_out[0].astype(BF16), wgu, w_down[0].astype(BF16), _pick_tile(seq, 512))
```

```python
import functools

import jax
import jax.numpy as jnp
import numpy as np
from jax import lax
from jax.experimental import pallas as pl
from jax.experimental.pallas import tpu as pltpu

F32 = jnp.float32
BF16 = jnp.bfloat16

D_MODEL = 1024
ATTN_HEADS = 8
ATTN_KV_HEADS = 2
ATTN_HEAD_DIM = 64
ATTN_BLOCK = 128
ROT_DIM = ATTN_HEAD_DIM // 4
ROPE_THETA = 500000.0
ATTN_Q_WIDTH = ATTN_HEADS * ATTN_HEAD_DIM
ATTN_KV_WIDTH = ATTN_KV_HEADS * ATTN_HEAD_DIM
DN_HEADS = 4
DN_DIM = 128
DN_CONV = 4
DN_CHUNK = 64
DN_SUB = 8
DN_GROUP = 2
DN_SEQS = 2
DN_WIDTH = DN_HEADS * DN_DIM
DN_CONV_DIM = 3 * DN_WIDTH
FFN_HIDDEN = 2816
FFN_SPLIT = 1536
NORM_EPS = 1e-6
LANES = 128

COL_QKV = 0
COL_DN = COL_QKV + ATTN_Q_WIDTH + 2 * ATTN_KV_WIDTH
COL_BA = COL_DN + DN_CONV_DIM
COL_Z = COL_BA + LANES
COL_GATES = COL_Z + DN_WIDTH
IN_PACKED = COL_GATES + 2 * D_MODEL

VMEM_LIMIT = 56 * 1024 * 1024


def _sigmoid(x):
    return 0.5 * jnp.tanh(0.5 * x) + 0.5


def _sigmoid_of_twice(xh):
    return 0.5 * jnp.tanh(xh) + 0.5


def _silu_of_twice(xh):
    return xh * jnp.tanh(xh) + xh


def _dot(a, b):
    return jnp.dot(a, b, preferred_element_type=F32)


def _dot_nt(a, b):
    return lax.dot_general(a, b, (((1,), (1,)), ((), ())), preferred_element_type=F32)


def _software_pipeline(stages):
    pending = None
    for issue, finish in stages:
        value = issue()
        if pending is not None:
            pending[0](pending[1])
        pending = (finish, value)
    pending[0](pending[1])


def _staged_pipeline(items, stages):
    depth = len(stages)
    values = {}
    for step in range(len(items) + depth - 1):
        for k in range(depth):
            i = step - k
            if 0 <= i < len(items):
                values[i] = stages[k](items[i], values.get(i))
                if k == depth - 1:
                    del values[i]


def _const_spec(shape):
    nd = len(shape)
    return pl.BlockSpec(shape, lambda *_: (0,) * nd, pipeline_mode=pl.Buffered(1))


def _params(n_grid):
    return pltpu.CompilerParams(dimension_semantics=("arbitrary",) * n_grid,
                                vmem_limit_bytes=VMEM_LIMIT)


def _ada_kernel(c_ref, w_ref, b_ref, o_ref):
    c = c_ref[...]
    cond = c * _sigmoid(c)
    o_ref[...] = jnp.dot(cond, w_ref[...], preferred_element_type=F32,
                         precision=lax.Precision.HIGHEST) + b_ref[...]


def _ada(c, w, b):
    bsz, d = c.shape
    n = w.shape[1]
    bn = 1536
    return pl.pallas_call(
        _ada_kernel,
        grid=(n // bn,),
        in_specs=[pl.BlockSpec((bsz, d), lambda j: (0, 0)),
                  pl.BlockSpec((d, bn), lambda j: (0, j)),
                  pl.BlockSpec((1, bn), lambda j: (0, j))],
        out_specs=pl.BlockSpec((bsz, bn), lambda j: (0, j)),
        out_shape=jax.ShapeDtypeStruct((bsz, n), F32),
        compiler_params=_params(1),
        name="ada",
    )(c, w, b.reshape(1, n))


def _inproj_kernel(tiles_per_seq, x_ref, mod_ref, pos_ref, g1_ref, w_ref, qkg_ref, freq_ref,
                   convw_ref, alog_ref, dtb_ref,
                   q_ref, k_ref, v_ref, dqkv_ref, bg_ref, zs_ref, sg_ref,
                   xc_ref, h_ref):
    i = pl.program_id(0)
    tt = x_ref.shape[1]
    t = jnp.maximum(i - 1, 0) % tiles_per_seq
    slot_cur = (i + 1) % 2
    slot_new = i % 2

    @pl.when(i == 0)
    def _():
        h_ref[1] = jnp.zeros(h_ref.shape[1:], BF16)

    h = h_ref[slot_cur]

    def normalise_next_tile():
        x = x_ref[0]
        ms = jnp.mean(x * x, axis=-1, keepdims=True)
        gain_scale = g1_ref[...] * (1.0 + mod_ref[0, 1:2, :])
        h_ref[slot_new] = (x * lax.rsqrt(ms + NORM_EPS) * gain_scale + mod_ref[0, 0:1, :]).astype(BF16)

    @pl.when(t == 0)
    def _():
        xc_ref[0:8, :] = jnp.zeros((8, DN_CONV_DIM), F32)

    ang = jnp.tile(freq_ref[...], (1, tt // LANES)) * pos_ref[0].astype(F32)
    cos_t = jnp.cos(ang)
    sin_t = jnp.sin(ang)
    one_t = jnp.ones_like(cos_t)
    zero_t = jnp.zeros_like(cos_t)

    def lane_table(first, second, fill):
        head = [first, second] + [fill] * (ATTN_HEAD_DIM // 8 - 2)
        return jnp.concatenate(head * (LANES // ATTN_HEAD_DIM), axis=0).T

    cos = lane_table(cos_t, cos_t, one_t)
    sin_signed = lane_table(-sin_t, sin_t, zero_t)
    half = ROT_DIM // 2
    first_half = (lax.broadcasted_iota(jnp.int32, (tt, LANES), 1) % ROT_DIM) < half

    def project(lo, hi):
        return lambda: _dot(h, w_ref[:, lo:hi])

    low_head = lax.broadcasted_iota(jnp.int32, (tt, LANES), 1) < ATTN_HEAD_DIM

    def finish_qkv(qkv):
        def head_rms_norm(u, gain):
            sq = u * u
            total = jnp.sum(sq, axis=-1, keepdims=True)
            low = jnp.sum(jnp.where(low_head, sq, 0.0), axis=-1, keepdims=True)
            mean_sq = jnp.where(low_head, low, total - low) * (1.0 / ATTN_HEAD_DIM)
            return u * lax.rsqrt(mean_sq + NORM_EPS) * gain

        def rope(u):
            partner = jnp.where(first_half, pltpu.roll(u, LANES - half, axis=1), pltpu.roll(u, half, axis=1))
            return u * cos + partner * sin_signed

        for i in range(ATTN_Q_WIDTH // LANES):
            sl = slice(i * LANES, (i + 1) * LANES)
            qn = head_rms_norm(qkv[:, sl], qkg_ref[0:1, sl])
            q_ref[0, :, sl] = (rope(qn) * (ATTN_HEAD_DIM ** -0.5)).astype(BF16)
        kn = head_rms_norm(qkv[:, ATTN_Q_WIDTH:ATTN_Q_WIDTH + ATTN_KV_WIDTH], qkg_ref[1:2, :ATTN_KV_WIDTH])
        k_ref[0] = rope(kn).astype(BF16)
        v_ref[0] = qkv[:, ATTN_Q_WIDTH + ATTN_KV_WIDTH:].astype(BF16)

    def finish_conv(group):
        c0 = group * DN_WIDTH

        def finish(proj):
            cols = slice(c0, c0 + DN_WIDTH)
            carry = xc_ref[:, cols]
            xc_ref[:, cols] = proj[tt - 8:, :]
            row8 = lax.broadcasted_iota(jnp.int32, (8, DN_WIDTH), 0)
            conv = convw_ref[DN_CONV - 1:DN_CONV, cols] * proj
            slabs = [carry] + [proj[8 * g:8 * (g + 1), :] for g in range(tt // 8)]
            for s in range(1, DN_CONV):
                rot = [pltpu.roll(slab, s, axis=0) for slab in slabs]
                shifted = jnp.concatenate(
                    [jnp.where(row8 < s, rot[g], rot[g + 1]) for g in range(tt // 8)], axis=0)
                conv = conv + convw_ref[DN_CONV - 1 - s:DN_CONV - s, cols] * shifted
            act = _silu_of_twice(conv)
            if group == 2:
                dqkv_ref[0, :, cols] = act.astype(BF16)
                return
            scale = DN_DIM ** -0.5 if group == 0 else 1.0
            for i in range(DN_HEADS):
                u = act[:, i * DN_DIM:(i + 1) * DN_DIM]
                un = u * (lax.rsqrt(jnp.sum(u * u, axis=-1, keepdims=True) + NORM_EPS) * scale)
                dqkv_ref[0, :, c0 + i * DN_DIM:c0 + (i + 1) * DN_DIM] = un.astype(BF16)

        return finish

    def finish_ba_z(baz):
        ba = baz[:, :LANES]
        z = baz[:, LANES:]
        sp_in = ba + dtb_ref[...]
        softplus = jnp.maximum(sp_in, 0.0) + jnp.log(1.0 + jnp.exp(-jnp.abs(sp_in)))
        g = -jnp.exp(alog_ref[...]) * softplus
        lane = lax.broadcasted_iota(jnp.int32, ba.shape, 1)
        bg_ref[0] = jnp.where(lane < DN_HEADS, _sigmoid(ba), g)
        zs_ref[0] = _silu_of_twice(z).astype(BF16)

    def finish_gate(i):
        def finish(gate):
            sg_ref[0, :, i * D_MODEL:(i + 1) * D_MODEL] = gate.astype(BF16)
            if i == 0:
                normalise_next_tile()
        return finish

    conv = [(project(COL_DN + g * DN_WIDTH, COL_DN + (g + 1) * DN_WIDTH), finish_conv(g)) for g in range(3)]
    gate = [(project(COL_GATES + i * D_MODEL, COL_GATES + (i + 1) * D_MODEL), finish_gate(i)) for i in range(2)]
    _software_pipeline([(project(COL_QKV, COL_DN), finish_qkv), conv[0], gate[0], conv[1], gate[1], conv[2],
                        (project(COL_BA, COL_GATES), finish_ba_z)])


def _inproj(x, mod, pos, g1, w_packed, qkg, freq, convw, alog, dtb, tt):
    bsz, seq, d = x.shape
    tps = seq // tt
    n = bsz * tps

    def cur(i):
        return jnp.minimum(i, n - 1)

    def prev(i):
        return jnp.maximum(i - 1, 0)

    def tile(width):
        return pl.BlockSpec((1, tt, width), lambda i: (prev(i) // tps, prev(i) % tps, 0))

    out_shapes = [
        jax.ShapeDtypeStruct((bsz, seq, ATTN_Q_WIDTH), BF16),
        jax.ShapeDtypeStruct((bsz, seq, ATTN_KV_WIDTH), BF16),
        jax.ShapeDtypeStruct((bsz, seq, ATTN_KV_WIDTH), BF16),
        jax.ShapeDtypeStruct((bsz, seq, DN_CONV_DIM), BF16),
        jax.ShapeDtypeStruct((bsz, seq, LANES), F32),
        jax.ShapeDtypeStruct((bsz, seq, DN_WIDTH), BF16),
        jax.ShapeDtypeStruct((bsz, seq, 2 * D_MODEL), BF16),
    ]
    return pl.pallas_call(
        functools.partial(_inproj_kernel, tps),
        grid=(n + 1,),
        in_specs=[pl.BlockSpec((1, tt, d), lambda i: (cur(i) // tps, cur(i) % tps, 0)),
                  pl.BlockSpec((1, 6, d), lambda i: (cur(i) // tps, 0, 0)),
                  pl.BlockSpec((1, 1, tt), lambda i: (prev(i) // tps, 0, prev(i) % tps)),
                  _const_spec((1, d)),
                  _const_spec(w_packed.shape),
                  _const_spec(qkg.shape),
                  _const_spec(freq.shape),
                  _const_spec(convw.shape),
                  _const_spec(alog.shape),
                  _const_spec(dtb.shape)],
        out_specs=[tile(ATTN_Q_WIDTH), tile(ATTN_KV_WIDTH), tile(ATTN_KV_WIDTH), tile(DN_CONV_DIM),
                   tile(LANES), tile(DN_WIDTH), tile(2 * D_MODEL)],
        out_shape=out_shapes,
        scratch_shapes=[pltpu.VMEM((8, DN_CONV_DIM), F32),
                        pltpu.VMEM((2, tt, d), BF16)],
        compiler_params=_params(1),
        name="inproj",
    )(x, mod, pos, g1, w_packed, qkg, freq, convw, alog, dtb)


def _attn_kernel(q_ref, k_ref, v_ref, kp_ref, vp_ref, sink_ref, o_ref, kw_ref, vw_ref):
    t = pl.program_id(1)
    tt = q_ref.shape[1]
    blk = ATTN_BLOCK
    hd = ATTN_HEAD_DIM

    kw_ref[blk:, :] = k_ref[0]
    vw_ref[blk:, :] = v_ref[0]

    @pl.when(t == 0)
    def _():
        kw_ref[:blk, :] = jnp.zeros((blk, ATTN_KV_WIDTH), BF16)
        vw_ref[:blk, :] = jnp.zeros((blk, ATTN_KV_WIDTH), BF16)

    @pl.when(t > 0)
    def _():
        kw_ref[:blk, :] = kp_ref[0]
        vw_ref[:blk, :] = vp_ref[0]

    row = lax.broadcasted_iota(jnp.int32, (blk, 2 * blk), 0)
    col = lax.broadcasted_iota(jnp.int32, (blk, 2 * blk), 1)
    band = (col > row) & (col <= row + blk)
    win = tt + blk
    lo_kv = lax.broadcasted_iota(jnp.int32, (win, LANES), 1) < hd
    lo_q = lax.broadcasted_iota(jnp.int32, (blk, LANES), 1) < hd
    ones_lo = jnp.where(lo_kv, 1.0, 0.0).astype(BF16)
    ones_hi = jnp.where(lo_kv, 0.0, 1.0).astype(BF16)

    kwin = kw_ref[...].astype(F32)
    vwin = vw_ref[...].astype(F32)
    kswp = pltpu.roll(kwin, hd, axis=1)
    vswp = pltpu.roll(vwin, hd, axis=1)
    k_lo, k_hi, v_lo, v_hi = [], [], [], []
    for g in range(ATTN_KV_HEADS):
        k_lo.append(jnp.where(lo_kv, kwin if g == 0 else kswp, 0.0).astype(BF16))
        k_hi.append(jnp.where(lo_kv, 0.0, kswp if g == 0 else kwin).astype(BF16))
        v_lo.append(jnp.concatenate(
            [jnp.where(lo_kv, vwin if g == 0 else vswp, 0.0).astype(BF16), ones_lo], axis=1))
        v_hi.append(jnp.concatenate(
            [jnp.where(lo_kv, 0.0, vswp if g == 0 else vwin).astype(BF16), ones_hi], axis=1))

    valid_first = band & (col >= jnp.where(t == 0, blk, 0))
    items = [(j, p) for j in range(tt // blk) for p in range(ATTN_HEADS // 2)]

    def scores(item, _):
        j, p = item
        rows = slice(j * blk, (j + 2) * blk)
        return _dot_nt(q_ref[0, j * blk:(j + 1) * blk, p * LANES:(p + 1) * LANES],
                       jnp.concatenate([k_lo[p // 2][rows], k_hi[p // 2][rows]], axis=0))

    def softmax(item, s):
        j, p = item
        valid = valid_first if j == 0 else band
        ps, ms = [], []
        for e in range(2):
            se = jnp.where(valid, s[:, e * 2 * blk:(e + 1) * 2 * blk], -jnp.inf)
            m = jnp.maximum(jnp.max(se, axis=-1, keepdims=True), sink_ref[p:p + 1, e * hd:e * hd + 1])
            ps.append(jnp.exp(se - m).astype(BF16))
            ms.append(m)
        return jnp.concatenate(ps, axis=1), jnp.where(lo_q, ms[0], ms[1])

    def values(item, soft):
        j, p = item
        rows = slice(j * blk, (j + 2) * blk)
        pv = _dot(soft[0], jnp.concatenate([v_lo[p // 2][rows], v_hi[p // 2][rows]], axis=0))
        return pv, soft[1]

    def finish(item, res):
        j, p = item
        pv, m_pair = res
        denom = pv[:, LANES:] + jnp.exp(sink_ref[p:p + 1, :] - m_pair)
        o_ref[0, j * blk:(j + 1) * blk, p * LANES:(p + 1) * LANES] = (pv[:, :LANES] / denom).astype(BF16)

    _staged_pipeline(items, [scores, softmax, values, finish])


def _attn(q, k, v, sink_lanes, tt):
    bsz, seq, _ = q.shape
    nprev = tt // ATTN_BLOCK

    def tile(width):
        return pl.BlockSpec((1, tt, width), lambda b, t: (b, t, 0))

    def prev(width):
        return pl.BlockSpec((1, ATTN_BLOCK, width),
                            lambda b, t: (b, jnp.maximum(t * nprev - 1, 0), 0))

    return pl.pallas_call(
        _attn_kernel,
        grid=(bsz, seq // tt),
        in_specs=[tile(ATTN_Q_WIDTH), tile(ATTN_KV_WIDTH), tile(ATTN_KV_WIDTH),
                  prev(ATTN_KV_WIDTH), prev(ATTN_KV_WIDTH),
                  _const_spec(sink_lanes.shape)],
        out_specs=tile(ATTN_Q_WIDTH),
        out_shape=jax.ShapeDtypeStruct((bsz, seq, ATTN_Q_WIDTH), BF16),
        scratch_shapes=[pltpu.VMEM((tt + ATTN_BLOCK, ATTN_KV_WIDTH), BF16),
                        pltpu.VMEM((tt + ATTN_BLOCK, ATTN_KV_WIDTH), BF16)],
        compiler_params=_params(2),
        name="attn",
    )(q, k, v, k, v, sink_lanes)


def _gdn_kernel(dqkv_ref, bg_ref, zs_ref, gain_ref, bcat_ref, o_ref, state_ref, gc_ref):
    t = pl.program_id(1)
    nb, tt = dqkv_ref.shape[0], dqkv_ref.shape[1]
    c = DN_CHUNK
    n_chunks = tt // c

    @pl.when(t == 0)
    def _():
        state_ref[...] = jnp.zeros(state_ref.shape, F32)

    rin = lax.broadcasted_iota(jnp.int32, (tt, LANES), 0) % c
    gcts = []
    for bi in range(nb):
        gc = bg_ref[bi]
        step = 1
        while step < c:
            gc = gc + jnp.where(rin >= step, pltpu.roll(gc, step, axis=0), 0.0)
            step *= 2
        gc_ref[bi] = gc
        gcts.append(gc.T)

    def locate(ch):
        return ch % nb, (ch // nb) * c

    ri = lax.broadcasted_iota(jnp.int32, (c, c), 0)
    ci = lax.broadcasted_iota(jnp.int32, (c, c), 1)
    incl = ri >= ci
    strict = ri > ci
    sb = DN_SUB
    diag_blocks = (ri // sb) == (ci // sb)
    merge_masks = []
    width = sb
    while width < c:
        merge_masks.append(((ri // (2 * width)) == (ci // (2 * width))) & ((ri // width) != (ci // width)))
        width *= 2
    lane_blk = lax.broadcasted_iota(jnp.int32, (sb, c), 1) // sb
    sub_p = lax.broadcasted_iota(jnp.int32, (sb, LANES), 0)
    lane_p = lax.broadcasted_iota(jnp.int32, (sb, LANES), 1)
    eye_packed = jnp.where(sub_p == lane_p % sb, 1.0, 0.0).astype(F32)

    heads = range(DN_HEADS)

    def load_and_gram(group, _):
        w = {}
        for ch, h in group:
            bi, r0 = locate(ch)
            q = dqkv_ref[bi, r0:r0 + c, h * DN_DIM:(h + 1) * DN_DIM].astype(F32)
            k = dqkv_ref[bi, r0:r0 + c, DN_WIDTH + h * DN_DIM:DN_WIDTH + (h + 1) * DN_DIM].astype(F32)
            kb = k * bg_ref[bi, r0:r0 + c, h:h + 1]
            a = _dot_nt(jnp.concatenate([kb, q], axis=0).astype(BF16), k.astype(BF16))
            w[ch, h] = dict(q=q, k=k, kb=kb, a=a)
        return w

    def intra_chunk(group, w):
        packed = []
        for ch, h in group:
            bi, r0 = locate(ch)
            wk = w[ch, h]
            v = dqkv_ref[bi, r0:r0 + c, 2 * DN_WIDTH + h * DN_DIM:2 * DN_WIDTH + (h + 1) * DN_DIM].astype(F32)
            beta = bg_ref[bi, r0:r0 + c, h:h + 1]
            gcc = gc_ref[bi, r0:r0 + c, DN_HEADS + h:DN_HEADS + h + 1]
            gcr = gcts[bi][DN_HEADS + h:DN_HEADS + h + 1, r0:r0 + c]
            g_last = gcc[c - 1:c, :]
            eg = jnp.exp(gcc)
            decay = jnp.exp(jnp.where(incl, gcc - gcr, -jnp.inf))
            l_mat = jnp.where(strict, wk["a"][:c] * decay, 0.0)
            d8 = jnp.zeros((sb, c), F32)
            for g in range(c // sb):
                d8 = d8 + jnp.where(lane_blk == g, l_mat[g * sb:(g + 1) * sb, :], 0.0)
            packed.append(d8)
            w[ch, h] = dict(
                l=l_mat, a_intra=(wk["a"][c:] * decay).astype(BF16),
                rhs=jnp.concatenate([v * beta, wk["kb"] * eg], axis=1).astype(BF16),
                q_dec=wk["q"] * eg, k_dec_t=(wk["k"] * jnp.exp(g_last - gcc)).T.astype(BF16),
                d_chunk=jnp.exp(g_last))
        pall = jnp.concatenate(
            [jnp.concatenate(packed[2 * i:2 * i + 2], axis=1) for i in range(len(packed) // 2)], axis=0)
        w["packed"] = pall
        p_hi = pall.astype(BF16)
        p_lo = (pall - p_hi.astype(F32)).astype(BF16)
        spread = _dot(p_hi, bcat_ref[...]) + _dot(p_lo, bcat_ref[...])
        w["cols"] = [spread[:, j * LANES:(j + 1) * LANES] for j in range(sb - 1)]
        return w

    def diag_inverse(group, w):
        pall = w["packed"]
        n_pairs = pall.shape[0] // sb
        cols = w["cols"]
        tp = jnp.concatenate([eye_packed] * n_pairs, axis=0)
        for j in range(sb - 1):
            row_j = jnp.concatenate(
                [jnp.broadcast_to(tp[p * sb + j:p * sb + j + 1, :], (sb, LANES)) for p in range(n_pairs)], axis=0)
            tp = tp - cols[j] * row_j
        for i, key in enumerate(group):
            blocks = tp[(i // 2) * sb:(i // 2 + 1) * sb, (i % 2) * c:(i % 2 + 1) * c]
            w[key]["tinv"] = jnp.where(diag_blocks, jnp.tile(blocks, (c // sb, 1)), 0.0)
        return w

    def lower_rows(mat, width):
        return jnp.concatenate([mat[b * 2 * width + width:(b + 1) * 2 * width] for b in range(c // (2 * width))],
                               axis=0)

    def per_key(fn):
        def stage(group, w):
            for key in group:
                fn(w[key])
            return w
        return stage

    def merge_first(mask, width):
        def fn(wk):
            wk["tb"] = wk["tinv"].astype(BF16)
            wk["half"] = _dot(lower_rows(wk["tinv"], width).astype(BF16),
                              jnp.where(mask, wk["l"], 0.0).astype(BF16)).astype(BF16)
        return per_key(fn)

    def merge_second(width):
        def fn(wk):
            t = wk["tinv"]
            new_low = lower_rows(t, width) - _dot(wk["half"], wk["tb"])
            parts = []
            for b in range(c // (2 * width)):
                parts += [t[b * 2 * width:b * 2 * width + width], new_low[b * width:(b + 1) * width]]
            wk["tinv"] = jnp.concatenate(parts, axis=0)
        return per_key(fn)

    def solve(wk):
        wk["uw"] = _dot(wk["tinv"].astype(BF16), wk["rhs"]).astype(BF16)

    def state_free(wk):
        both = _dot(jnp.concatenate([wk["a_intra"], wk["k_dec_t"]], axis=0), wk["uw"])
        wk["a_uw"] = both[:c]
        wk["k_uw"] = both[c:]
        wk["lhs"] = jnp.concatenate([wk["q_dec"] - both[:c, DN_DIM:], both[c:, DN_DIM:]],
                                    axis=0).astype(BF16)

    state = {(bi, h): state_ref[bi, h] for bi in range(nb) for h in heads}

    def recurrence(group, w):
        for slot in sorted({ch for ch, _ in group}):
            recurrence_slot([key for key in group if key[0] == slot], w)

    def recurrence_slot(keys, w):
        prod = {}
        for ch, h in keys:
            prod[ch, h] = _dot(w[ch, h]["lhs"], state[locate(ch)[0], h].astype(BF16))
        for ch, h in keys:
            bi, r0 = locate(ch)
            wk = w[ch, h]
            o = prod[ch, h][:c] + wk["a_uw"][:, :DN_DIM]
            state[bi, h] = state[bi, h] * wk["d_chunk"] - prod[ch, h][c:] + wk["k_uw"][:, :DN_DIM]
            on = o * lax.rsqrt(jnp.mean(o * o, axis=-1, keepdims=True) + NORM_EPS) * gain_ref[...]
            zs = zs_ref[bi, r0:r0 + c, h * DN_DIM:(h + 1) * DN_DIM].astype(F32)
            o_ref[bi, r0:r0 + c, h * DN_DIM:(h + 1) * DN_DIM] = (on * zs).astype(BF16)

    stages = [load_and_gram, intra_chunk, diag_inverse]
    for level, mask in enumerate(merge_masks):
        stages += [merge_first(mask, sb << level), merge_second(sb << level)]
    stages += [per_key(solve), per_key(state_free), recurrence]
    n_slots = nb * n_chunks
    size = min(DN_GROUP, n_slots)
    groups = [[(ch, h) for ch in range(g, g + size) for h in heads] for g in range(0, n_slots, size)]
    _staged_pipeline(groups, stages)
    for key, value in state.items():
        state_ref[key] = value


def _gdn_spread_matrix():
    sb = DN_SUB
    mat = np.zeros((LANES, (sb - 1) * LANES), np.float32)
    for j in range(sb - 1):
        for m in range(LANES // sb):
            mat[sb * m + j, j * LANES + sb * m:j * LANES + sb * (m + 1)] = 1.0
    return mat


def _gdn(dqkv, bg, zs, gain, tt):
    bsz, seq, _ = dqkv.shape
    nb = DN_SEQS if bsz % DN_SEQS == 0 else 1

    def tile(width):
        return pl.BlockSpec((nb, tt, width), lambda b, t: (b, t, 0))

    bcat = jnp.asarray(_gdn_spread_matrix(), BF16)
    return pl.pallas_call(
        _gdn_kernel,
        grid=(bsz // nb, seq // tt),
        in_specs=[tile(DN_CONV_DIM), tile(LANES), tile(DN_WIDTH), _const_spec(gain.shape),
                  _const_spec(bcat.shape)],
        out_specs=tile(DN_WIDTH),
        out_shape=jax.ShapeDtypeStruct((bsz, seq, DN_WIDTH), BF16),
        scratch_shapes=[pltpu.VMEM((nb, DN_HEADS, DN_DIM, DN_DIM), F32),
                        pltpu.VMEM((nb, tt, LANES), F32)],
        compiler_params=_params(2),
        name="gdn",
    )(dqkv, bg, zs, gain, bcat)


def _mix_ffn_kernel(x_ref, oa_ref, od_ref, gt_ref, mod_ref, g2_ref, wb_ref, wo_ref, wgu_ref, wd_ref, o_ref):
    tt = x_ref.shape[1]
    half_rows = tt // 2
    gate1 = mod_ref[0, 2:3, :]
    shift2 = mod_ref[0, 3:4, :]
    gain_scale2 = g2_ref[...] * (1.0 + mod_ref[0, 4:5, :])
    gate2 = mod_ref[0, 5:6, :]
    hid = [(0, FFN_SPLIT), (FFN_SPLIT, FFN_HIDDEN)]

    def rows(i):
        return slice(i * half_rows, (i + 1) * half_rows)

    def branch_proj(i, _):
        return dict(ya=_dot(oa_ref[0, rows(i), :], wb_ref[:ATTN_Q_WIDTH, :]),
                    yd=_dot(od_ref[0, rows(i), :], wb_ref[ATTN_Q_WIDTH:, :]))

    def merge(i, w):
        sa = _sigmoid_of_twice(gt_ref[0, rows(i), :D_MODEL].astype(F32))
        sd = _sigmoid_of_twice(gt_ref[0, rows(i), D_MODEL:].astype(F32))
        return dict(merged=(sa * w["ya"] + sd * w["yd"]).astype(BF16))

    def out_proj(i, w):
        return dict(out=_dot(w["merged"], wo_ref[...]))

    def residual_norm(i, w):
        x1 = x_ref[0, rows(i), :] + gate1 * w["out"]
        ms = jnp.mean(x1 * x1, axis=-1, keepdims=True)
        return dict(x1=x1, h=(x1 * lax.rsqrt(ms + NORM_EPS) * gain_scale2 + shift2).astype(BF16))

    def gate_up(part):
        lo, hi = hid[part]

        def stage(i, w):
            w[f"g{part}"] = _dot(w["h"], wgu_ref[:, lo:hi])
            w[f"u{part}"] = _dot(w["h"], wgu_ref[:, FFN_HIDDEN + lo:FFN_HIDDEN + hi])
            return w
        return stage

    def activation(part):
        def stage(i, w):
            w[f"act{part}"] = (_silu_of_twice(w.pop(f"g{part}")) * w.pop(f"u{part}")).astype(BF16)
            return w
        return stage

    def down_proj(i, w):
        w["y"] = sum(_dot(w[f"act{p}"], wd_ref[lo:hi, :]) for p, (lo, hi) in enumerate(hid))
        return w

    def finish(i, w):
        o_ref[0, rows(i), :] = w["x1"] + gate2 * w["y"]

    _staged_pipeline([0, 1], [branch_proj, merge, out_proj, residual_norm, gate_up(0), activation(0),
                              gate_up(1), activation(1), down_proj, finish])


def _mix_ffn(x, oa, od, gt, mod, g2, wb, wo, wgu, wd, tt):
    bsz, seq, d = x.shape

    def tile(width):
        return pl.BlockSpec((1, tt, width), lambda b, t: (b, t, 0))

    return pl.pallas_call(
        _mix_ffn_kernel,
        grid=(bsz, seq // tt),
        in_specs=[tile(d), tile(ATTN_Q_WIDTH), tile(DN_WIDTH), tile(2 * d),
                  pl.BlockSpec((1, 6, d), lambda b, t: (b, 0, 0)), _const_spec((1, d)),
                  _const_spec(wb.shape), _const_spec(wo.shape), _const_spec(wgu.shape), _const_spec(wd.shape)],
        out_specs=tile(d),
        out_shape=jax.ShapeDtypeStruct((bsz, seq, d), F32),
        compiler_params=_params(2),
        name="mix_ffn",
    )(x, oa, od, gt, mod, g2, wb, wo, wgu, wd)


def _rope_freqs():
    inv_freq = ROPE_THETA ** (-jnp.arange(0, ROT_DIM, 2, dtype=F32) / ROT_DIM)
    return jnp.broadcast_to(inv_freq[:, None], (ROT_DIM // 2, LANES))


def _pick_tile(seq, want):
    tt = min(want, seq)
    assert seq % tt == 0 and tt % ATTN_BLOCK == 0, (seq, tt)
    return tt


def kernel(x, c, positions, ada_w, ada_b, norm1_g, w_in, conv_w, q_norm_g, k_norm_g, sinks, a_log,
           dt_bias, dn_norm_g, w_branch, w_out, norm2_g, w_gate_up, w_down):
    bsz, seq, d = x.shape
    assert d == D_MODEL and ada_w.shape[0] == 1, "single-layer kernel"
    n_in = ATTN_Q_WIDTH + 2 * ATTN_KV_WIDTH + DN_CONV_DIM

    mod = _ada(c, ada_w[0], ada_b[0]).reshape(bsz, 6, d)

    w = w_in[0]
    w_packed = jnp.concatenate(
        [w[:, :n_in], w[:, n_in:n_in + 2 * DN_HEADS],
         jnp.zeros((d, LANES - 2 * DN_HEADS), w.dtype), 0.5 * w[:, n_in + 2 * DN_HEADS:]],
        axis=1).astype(BF16)
    assert w_packed.shape[1] == IN_PACKED
    wgu = jnp.concatenate([0.5 * w_gate_up[0][:, :FFN_HIDDEN], w_gate_up[0][:, FFN_HIDDEN:]],
                          axis=1).astype(BF16)

    qkg = jnp.stack([jnp.tile(q_norm_g[0], ATTN_HEADS),
                     jnp.pad(jnp.tile(k_norm_g[0], ATTN_KV_HEADS), (0, ATTN_Q_WIDTH - ATTN_KV_WIDTH))])
    freq = _rope_freqs()
    convw = 0.5 * conv_w[0].reshape(DN_CONV, DN_CONV_DIM)
    pad4 = (DN_HEADS, LANES - 2 * DN_HEADS)
    alog = jnp.pad(a_log[0], pad4).reshape(1, LANES)
    dtb = jnp.pad(dt_bias[0], pad4).reshape(1, LANES)
    sink_lanes = jnp.repeat(sinks[0], ATTN_HEAD_DIM).reshape(ATTN_HEADS // 2, LANES)

    q, k, v, dqkv, bg, zs, sg = _inproj(
        x, mod, positions.reshape(bsz, 1, seq), norm1_g[0].reshape(1, d), w_packed, qkg, freq,
        convw, alog, dtb, _pick_tile(seq, 256))
    oa = _attn(q, k, v, sink_lanes, _pick_tile(seq, 4096))
    od = _gdn(dqkv, bg, zs, dn_norm_g[0].reshape(1, DN_DIM), _pick_tile(seq, 512))
    return _mix_ffn(x, oa, od, sg, mod, norm2_g[0].reshape(1, d), w_branch[0].astype(BF16),
                    w_out[0].astype(BF16), wgu, w_down[0].astype(BF16), _pick_tile(seq, 512))
```

```python
import functools

import jax
import jax.numpy as jnp
import numpy as np
from jax import lax
from jax.experimental import pallas as pl
from jax.experimental.pallas import tpu as pltpu

F32 = jnp.float32
BF16 = jnp.bfloat16

D_MODEL = 1024
ATTN_HEADS = 8
ATTN_KV_HEADS = 2
ATTN_HEAD_DIM = 64
ATTN_BLOCK = 128
ROT_DIM = ATTN_HEAD_DIM // 4
ROPE_THETA = 500000.0
ATTN_Q_WIDTH = ATTN_HEADS * ATTN_HEAD_DIM
ATTN_KV_WIDTH = ATTN_KV_HEADS * ATTN_HEAD_DIM
DN_HEADS = 4
DN_DIM = 128
DN_CONV = 4
DN_CHUNK = 64
DN_SUB = 8
DN_GROUP = 2
DN_SEQS = 2
DN_WIDTH = DN_HEADS * DN_DIM
DN_CONV_DIM = 3 * DN_WIDTH
FFN_HIDDEN = 2816
FFN_SPLIT = 1536
NORM_EPS = 1e-6
LANES = 128

COL_QKV = 0
COL_DN = COL_QKV + ATTN_Q_WIDTH + 2 * ATTN_KV_WIDTH
COL_BA = COL_DN + DN_CONV_DIM
COL_Z = COL_BA + LANES
COL_GATES = COL_Z + DN_WIDTH
IN_PACKED = COL_GATES + 2 * D_MODEL

VMEM_LIMIT = 56 * 1024 * 1024


def _sigmoid(x):
    return 0.5 * jnp.tanh(0.5 * x) + 0.5


def _sigmoid_of_twice(xh):
    return 0.5 * jnp.tanh(xh) + 0.5


def _silu_of_twice(xh):
    return xh * jnp.tanh(xh) + xh


def _dot(a, b):
    return jnp.dot(a, b, preferred_element_type=F32)


def _dot_nt(a, b):
    return lax.dot_general(a, b, (((1,), (1,)), ((), ())), preferred_element_type=F32)


def _software_pipeline(stages):
    pending = None
    for issue, finish in stages:
        value = issue()
        if pending is not None:
            pending[0](pending[1])
        pending = (finish, value)
    pending[0](pending[1])


def _staged_steps(items, stages, fine=False):
    depth = len(stages)
    values = {}
    for step in range(len(items) + depth - 1):
        for k in range(depth):
            i = step - k
            if 0 <= i < len(items):
                values[i] = stages[k](items[i], values.get(i))
                if k == depth - 1:
                    del values[i]
                if fine:
                    yield
        if not fine:
            yield


def _staged_pipeline(items, stages):
    for _ in _staged_steps(items, stages):
        pass


def _interleave(*pipelines):
    live = list(pipelines)
    while live:
        for gen in list(live):
            if next(gen, StopIteration) is StopIteration:
                live.remove(gen)


def _const_spec(shape):
    nd = len(shape)
    return pl.BlockSpec(shape, lambda *_: (0,) * nd, pipeline_mode=pl.Buffered(1))


def _params(n_grid):
    return pltpu.CompilerParams(dimension_semantics=("arbitrary",) * n_grid,
                                vmem_limit_bytes=VMEM_LIMIT)


def _ada_kernel(c_ref, w_ref, b_ref, o_ref):
    c = c_ref[...]
    cond = c * _sigmoid(c)
    o_ref[...] = jnp.dot(cond, w_ref[...], preferred_element_type=F32,
                         precision=lax.Precision.HIGHEST) + b_ref[...]


def _ada(c, w, b):
    bsz, d = c.shape
    n = w.shape[1]
    bn = 1536
    return pl.pallas_call(
        _ada_kernel,
        grid=(n // bn,),
        in_specs=[pl.BlockSpec((bsz, d), lambda j: (0, 0)),
                  pl.BlockSpec((d, bn), lambda j: (0, j)),
                  pl.BlockSpec((1, bn), lambda j: (0, j))],
        out_specs=pl.BlockSpec((bsz, bn), lambda j: (0, j)),
        out_shape=jax.ShapeDtypeStruct((bsz, n), F32),
        compiler_params=_params(1),
        name="ada",
    )(c, w, b.reshape(1, n))


def _inproj_kernel(tiles_per_seq, x_ref, mod_ref, pos_ref, g1_ref, w_ref, qkg_ref, freq_ref,
                   convw_ref, alog_ref, dtb_ref,
                   q_ref, k_ref, v_ref, dqkv_ref, bg_ref, zs_ref, sg_ref,
                   xc_ref, h_ref):
    i = pl.program_id(0)
    tt = x_ref.shape[1]
    t = jnp.maximum(i - 1, 0) % tiles_per_seq
    slot_cur = (i + 1) % 2
    slot_new = i % 2

    @pl.when(i == 0)
    def _():
        h_ref[1] = jnp.zeros(h_ref.shape[1:], BF16)

    h = h_ref[slot_cur]

    def normalise_next_tile():
        x = x_ref[0]
        ms = jnp.mean(x * x, axis=-1, keepdims=True)
        gain_scale = g1_ref[...] * (1.0 + mod_ref[0, 1:2, :])
        h_ref[slot_new] = (x * lax.rsqrt(ms + NORM_EPS) * gain_scale + mod_ref[0, 0:1, :]).astype(BF16)

    @pl.when(t == 0)
    def _():
        xc_ref[0:8, :] = jnp.zeros((8, DN_CONV_DIM), F32)

    ang = jnp.tile(freq_ref[...], (1, tt // LANES)) * pos_ref[0].astype(F32)
    cos_t = jnp.cos(ang)
    sin_t = jnp.sin(ang)
    one_t = jnp.ones_like(cos_t)
    zero_t = jnp.zeros_like(cos_t)

    def lane_table(first, second, fill):
        head = [first, second] + [fill] * (ATTN_HEAD_DIM // 8 - 2)
        return jnp.concatenate(head * (LANES // ATTN_HEAD_DIM), axis=0).T

    cos = lane_table(cos_t, cos_t, one_t)
    sin_signed = lane_table(-sin_t, sin_t, zero_t)
    half = ROT_DIM // 2
    first_half = (lax.broadcasted_iota(jnp.int32, (tt, LANES), 1) % ROT_DIM) < half

    def project(lo, hi):
        return lambda: _dot(h, w_ref[:, lo:hi])

    low_head = lax.broadcasted_iota(jnp.int32, (tt, LANES), 1) < ATTN_HEAD_DIM

    def finish_qkv(qkv):
        def head_rms_norm(u, gain):
            sq = u * u
            total = jnp.sum(sq, axis=-1, keepdims=True)
            low = jnp.sum(jnp.where(low_head, sq, 0.0), axis=-1, keepdims=True)
            mean_sq = jnp.where(low_head, low, total - low) * (1.0 / ATTN_HEAD_DIM)
            return u * lax.rsqrt(mean_sq + NORM_EPS) * gain

        def rope(u):
            partner = jnp.where(first_half, pltpu.roll(u, LANES - half, axis=1), pltpu.roll(u, half, axis=1))
            return u * cos + partner * sin_signed

        for i in range(ATTN_Q_WIDTH // LANES):
            sl = slice(i * LANES, (i + 1) * LANES)
            qn = head_rms_norm(qkv[:, sl], qkg_ref[0:1, sl])
            q_ref[0, :, sl] = (rope(qn) * (ATTN_HEAD_DIM ** -0.5)).astype(BF16)
        kn = head_rms_norm(qkv[:, ATTN_Q_WIDTH:ATTN_Q_WIDTH + ATTN_KV_WIDTH], qkg_ref[1:2, :ATTN_KV_WIDTH])
        k_ref[0] = rope(kn).astype(BF16)
        v_ref[0] = qkv[:, ATTN_Q_WIDTH + ATTN_KV_WIDTH:].astype(BF16)

    def finish_conv(group):
        c0 = group * DN_WIDTH

        def finish(proj):
            cols = slice(c0, c0 + DN_WIDTH)
            carry = xc_ref[:, cols]
            xc_ref[:, cols] = proj[tt - 8:, :]
            row8 = lax.broadcasted_iota(jnp.int32, (8, DN_WIDTH), 0)
            conv = convw_ref[DN_CONV - 1:DN_CONV, cols] * proj
            slabs = [carry] + [proj[8 * g:8 * (g + 1), :] for g in range(tt // 8)]
            for s in range(1, DN_CONV):
                rot = [pltpu.roll(slab, s, axis=0) for slab in slabs]
                shifted = jnp.concatenate(
                    [jnp.where(row8 < s, rot[g], rot[g + 1]) for g in range(tt // 8)], axis=0)
                conv = conv + convw_ref[DN_CONV - 1 - s:DN_CONV - s, cols] * shifted
            act = _silu_of_twice(conv)
            if group == 2:
                dqkv_ref[0, :, cols] = act.astype(BF16)
                return
            scale = DN_DIM ** -0.5 if group == 0 else 1.0
            for i in range(DN_HEADS):
                u = act[:, i * DN_DIM:(i + 1) * DN_DIM]
                un = u * (lax.rsqrt(jnp.sum(u * u, axis=-1, keepdims=True) + NORM_EPS) * scale)
                dqkv_ref[0, :, c0 + i * DN_DIM:c0 + (i + 1) * DN_DIM] = un.astype(BF16)

        return finish

    def finish_ba_z(baz):
        ba = baz[:, :LANES]
        z = baz[:, LANES:]
        sp_in = ba + dtb_ref[...]
        softplus = jnp.maximum(sp_in, 0.0) + jnp.log(1.0 + jnp.exp(-jnp.abs(sp_in)))
        g = -jnp.exp(alog_ref[...]) * softplus
        lane = lax.broadcasted_iota(jnp.int32, ba.shape, 1)
        bg_ref[0] = jnp.where(lane < DN_HEADS, _sigmoid(ba), g)
        zs_ref[0] = _silu_of_twice(z).astype(BF16)

    def finish_gate(i):
        def finish(gate):
            sg_ref[0, :, i * D_MODEL:(i + 1) * D_MODEL] = gate.astype(BF16)
            if i == 0:
                normalise_next_tile()
        return finish

    conv = [(project(COL_DN + g * DN_WIDTH, COL_DN + (g + 1) * DN_WIDTH), finish_conv(g)) for g in range(3)]
    gate = [(project(COL_GATES + i * D_MODEL, COL_GATES + (i + 1) * D_MODEL), finish_gate(i)) for i in range(2)]
    _software_pipeline([(project(COL_QKV, COL_DN), finish_qkv), conv[0], gate[0], conv[1], gate[1], conv[2],
                        (project(COL_BA, COL_GATES), finish_ba_z)])


def _inproj(x, mod, pos, g1, w_packed, qkg, freq, convw, alog, dtb, tt):
    bsz, seq, d = x.shape
    tps = seq // tt
    n = bsz * tps

    def cur(i):
        return jnp.minimum(i, n - 1)

    def prev(i):
        return jnp.maximum(i - 1, 0)

    def tile(width):
        return pl.BlockSpec((1, tt, width), lambda i: (prev(i) // tps, prev(i) % tps, 0))

    out_shapes = [
        jax.ShapeDtypeStruct((bsz, seq, ATTN_Q_WIDTH), BF16),
        jax.ShapeDtypeStruct((bsz, seq, ATTN_KV_WIDTH), BF16),
        jax.ShapeDtypeStruct((bsz, seq, ATTN_KV_WIDTH), BF16),
        jax.ShapeDtypeStruct((bsz, seq, DN_CONV_DIM), BF16),
        jax.ShapeDtypeStruct((bsz, seq, LANES), F32),
        jax.ShapeDtypeStruct((bsz, seq, DN_WIDTH), BF16),
        jax.ShapeDtypeStruct((bsz, seq, 2 * D_MODEL), BF16),
    ]
    return pl.pallas_call(
        functools.partial(_inproj_kernel, tps),
        grid=(n + 1,),
        in_specs=[pl.BlockSpec((1, tt, d), lambda i: (cur(i) // tps, cur(i) % tps, 0)),
                  pl.BlockSpec((1, 6, d), lambda i: (cur(i) // tps, 0, 0)),
                  pl.BlockSpec((1, 1, tt), lambda i: (prev(i) // tps, 0, prev(i) % tps)),
                  _const_spec((1, d)),
                  _const_spec(w_packed.shape),
                  _const_spec(qkg.shape),
                  _const_spec(freq.shape),
                  _const_spec(convw.shape),
                  _const_spec(alog.shape),
                  _const_spec(dtb.shape)],
        out_specs=[tile(ATTN_Q_WIDTH), tile(ATTN_KV_WIDTH), tile(ATTN_KV_WIDTH), tile(DN_CONV_DIM),
                   tile(LANES), tile(DN_WIDTH), tile(2 * D_MODEL)],
        out_shape=out_shapes,
        scratch_shapes=[pltpu.VMEM((8, DN_CONV_DIM), F32),
                        pltpu.VMEM((2, tt, d), BF16)],
        compiler_params=_params(1),
        name="inproj",
    )(x, mod, pos, g1, w_packed, qkg, freq, convw, alog, dtb)


def _attn_kernel(q_ref, k_ref, v_ref, kp_ref, vp_ref, sink_ref, o_ref, kw_ref, vw_ref):
    t = pl.program_id(1)
    tt = q_ref.shape[1]
    blk = ATTN_BLOCK
    hd = ATTN_HEAD_DIM

    kw_ref[blk:, :] = k_ref[0]
    vw_ref[blk:, :] = v_ref[0]

    @pl.when(t == 0)
    def _():
        kw_ref[:blk, :] = jnp.zeros((blk, ATTN_KV_WIDTH), BF16)
        vw_ref[:blk, :] = jnp.zeros((blk, ATTN_KV_WIDTH), BF16)

    @pl.when(t > 0)
    def _():
        kw_ref[:blk, :] = kp_ref[0]
        vw_ref[:blk, :] = vp_ref[0]

    row = lax.broadcasted_iota(jnp.int32, (blk, 2 * blk), 0)
    col = lax.broadcasted_iota(jnp.int32, (blk, 2 * blk), 1)
    band = (col > row) & (col <= row + blk)
    win = tt + blk
    lo_kv = lax.broadcasted_iota(jnp.int32, (win, LANES), 1) < hd
    lo_q = lax.broadcasted_iota(jnp.int32, (blk, LANES), 1) < hd
    ones_lo = jnp.where(lo_kv, 1.0, 0.0).astype(BF16)
    ones_hi = jnp.where(lo_kv, 0.0, 1.0).astype(BF16)

    kwin = kw_ref[...].astype(F32)
    vwin = vw_ref[...].astype(F32)
    kswp = pltpu.roll(kwin, hd, axis=1)
    vswp = pltpu.roll(vwin, hd, axis=1)
    k_lo, k_hi, v_lo, v_hi = [], [], [], []
    for g in range(ATTN_KV_HEADS):
        k_lo.append(jnp.where(lo_kv, kwin if g == 0 else kswp, 0.0).astype(BF16))
        k_hi.append(jnp.where(lo_kv, 0.0, kswp if g == 0 else kwin).astype(BF16))
        v_lo.append(jnp.concatenate(
            [jnp.where(lo_kv, vwin if g == 0 else vswp, 0.0).astype(BF16), ones_lo], axis=1))
        v_hi.append(jnp.concatenate(
            [jnp.where(lo_kv, 0.0, vswp if g == 0 else vwin).astype(BF16), ones_hi], axis=1))

    valid_first = band & (col >= jnp.where(t == 0, blk, 0))
    items = [(j, p) for j in range(tt // blk) for p in range(ATTN_HEADS // 2)]

    def scores(item, _):
        j, p = item
        rows = slice(j * blk, (j + 2) * blk)
        return _dot_nt(q_ref[0, j * blk:(j + 1) * blk, p * LANES:(p + 1) * LANES],
                       jnp.concatenate([k_lo[p // 2][rows], k_hi[p // 2][rows]], axis=0))

    def softmax(item, s):
        j, p = item
        valid = valid_first if j == 0 else band
        ps, ms = [], []
        for e in range(2):
            se = jnp.where(valid, s[:, e * 2 * blk:(e + 1) * 2 * blk], -jnp.inf)
            m = jnp.maximum(jnp.max(se, axis=-1, keepdims=True), sink_ref[p:p + 1, e * hd:e * hd + 1])
            ps.append(jnp.exp(se - m).astype(BF16))
            ms.append(m)
        return jnp.concatenate(ps, axis=1), jnp.where(lo_q, ms[0], ms[1])

    def values(item, soft):
        j, p = item
        rows = slice(j * blk, (j + 2) * blk)
        pv = _dot(soft[0], jnp.concatenate([v_lo[p // 2][rows], v_hi[p // 2][rows]], axis=0))
        return pv, soft[1]

    def finish(item, res):
        j, p = item
        pv, m_pair = res
        denom = pv[:, LANES:] + jnp.exp(sink_ref[p:p + 1, :] - m_pair)
        o_ref[0, j * blk:(j + 1) * blk, p * LANES:(p + 1) * LANES] = (pv[:, :LANES] / denom).astype(BF16)

    _staged_pipeline(items, [scores, softmax, values, finish])


def _attn(q, k, v, sink_lanes, tt):
    bsz, seq, _ = q.shape
    nprev = tt // ATTN_BLOCK

    def tile(width):
        return pl.BlockSpec((1, tt, width), lambda b, t: (b, t, 0))

    def prev(width):
        return pl.BlockSpec((1, ATTN_BLOCK, width),
                            lambda b, t: (b, jnp.maximum(t * nprev - 1, 0), 0))

    return pl.pallas_call(
        _attn_kernel,
        grid=(bsz, seq // tt),
        in_specs=[tile(ATTN_Q_WIDTH), tile(ATTN_KV_WIDTH), tile(ATTN_KV_WIDTH),
                  prev(ATTN_KV_WIDTH), prev(ATTN_KV_WIDTH),
                  _const_spec(sink_lanes.shape)],
        out_specs=tile(ATTN_Q_WIDTH),
        out_shape=jax.ShapeDtypeStruct((bsz, seq, ATTN_Q_WIDTH), BF16),
        scratch_shapes=[pltpu.VMEM((tt + ATTN_BLOCK, ATTN_KV_WIDTH), BF16),
                        pltpu.VMEM((tt + ATTN_BLOCK, ATTN_KV_WIDTH), BF16)],
        compiler_params=_params(2),
        name="attn",
    )(q, k, v, k, v, sink_lanes)


def _gdn_program(first_tile, dqkv_ref, bg_ref, zs_ref, gain_ref, bcat_ref, write_out, state_ref, gc_ref):
    nb, tt = dqkv_ref.shape[0], dqkv_ref.shape[1]
    c = DN_CHUNK
    n_chunks = tt // c

    @pl.when(first_tile)
    def _():
        state_ref[...] = jnp.zeros(state_ref.shape, F32)

    rin = lax.broadcasted_iota(jnp.int32, (tt, LANES), 0) % c
    gcts = []
    for bi in range(nb):
        gc = bg_ref[bi]
        step = 1
        while step < c:
            gc = gc + jnp.where(rin >= step, pltpu.roll(gc, step, axis=0), 0.0)
            step *= 2
        gc_ref[bi] = gc
        gcts.append(gc.T)

    def locate(ch):
        return ch % nb, (ch // nb) * c

    ri = lax.broadcasted_iota(jnp.int32, (c, c), 0)
    ci = lax.broadcasted_iota(jnp.int32, (c, c), 1)
    incl = ri >= ci
    strict = ri > ci
    sb = DN_SUB
    diag_blocks = (ri // sb) == (ci // sb)
    merge_masks = []
    width = sb
    while width < c:
        merge_masks.append(((ri // (2 * width)) == (ci // (2 * width))) & ((ri // width) != (ci // width)))
        width *= 2
    lane_blk = lax.broadcasted_iota(jnp.int32, (sb, c), 1) // sb
    sub_p = lax.broadcasted_iota(jnp.int32, (sb, LANES), 0)
    lane_p = lax.broadcasted_iota(jnp.int32, (sb, LANES), 1)
    eye_packed = jnp.where(sub_p == lane_p % sb, 1.0, 0.0).astype(F32)

    heads = range(DN_HEADS)

    def load_and_gram(group, _):
        w = {}
        for ch, h in group:
            bi, r0 = locate(ch)
            q = dqkv_ref[bi, r0:r0 + c, h * DN_DIM:(h + 1) * DN_DIM].astype(F32)
            k = dqkv_ref[bi, r0:r0 + c, DN_WIDTH + h * DN_DIM:DN_WIDTH + (h + 1) * DN_DIM].astype(F32)
            kb = k * bg_ref[bi, r0:r0 + c, h:h + 1]
            a = _dot_nt(jnp.concatenate([kb, q], axis=0).astype(BF16), k.astype(BF16))
            w[ch, h] = dict(q=q, k=k, kb=kb, a=a)
        return w

    def intra_chunk(group, w):
        packed = []
        for ch, h in group:
            bi, r0 = locate(ch)
            wk = w[ch, h]
            v = dqkv_ref[bi, r0:r0 + c, 2 * DN_WIDTH + h * DN_DIM:2 * DN_WIDTH + (h + 1) * DN_DIM].astype(F32)
            beta = bg_ref[bi, r0:r0 + c, h:h + 1]
            gcc = gc_ref[bi, r0:r0 + c, DN_HEADS + h:DN_HEADS + h + 1]
            gcr = gcts[bi][DN_HEADS + h:DN_HEADS + h + 1, r0:r0 + c]
            g_last = gcc[c - 1:c, :]
            eg = jnp.exp(gcc)
            decay = jnp.exp(jnp.where(incl, gcc - gcr, -jnp.inf))
            l_mat = jnp.where(strict, wk["a"][:c] * decay, 0.0)
            d8 = jnp.zeros((sb, c), F32)
            for g in range(c // sb):
                d8 = d8 + jnp.where(lane_blk == g, l_mat[g * sb:(g + 1) * sb, :], 0.0)
            packed.append(d8)
            w[ch, h] = dict(
                l=l_mat, a_intra=(wk["a"][c:] * decay).astype(BF16),
                rhs=jnp.concatenate([v * beta, wk["kb"] * eg], axis=1).astype(BF16),
                q_dec=wk["q"] * eg, k_dec_t=(wk["k"] * jnp.exp(g_last - gcc)).T.astype(BF16),
                d_chunk=jnp.exp(g_last))
        pall = jnp.concatenate(
            [jnp.concatenate(packed[2 * i:2 * i + 2], axis=1) for i in range(len(packed) // 2)], axis=0)
        w["packed"] = pall
        p_hi = pall.astype(BF16)
        p_lo = (pall - p_hi.astype(F32)).astype(BF16)
        spread = _dot(p_hi, bcat_ref[...]) + _dot(p_lo, bcat_ref[...])
        w["cols"] = [spread[:, j * LANES:(j + 1) * LANES] for j in range(sb - 1)]
        return w

    def diag_inverse(group, w):
        pall = w["packed"]
        n_pairs = pall.shape[0] // sb
        cols = w["cols"]
        tp = jnp.concatenate([eye_packed] * n_pairs, axis=0)
        for j in range(sb - 1):
            row_j = jnp.concatenate(
                [jnp.broadcast_to(tp[p * sb + j:p * sb + j + 1, :], (sb, LANES)) for p in range(n_pairs)], axis=0)
            tp = tp - cols[j] * row_j
        for i, key in enumerate(group):
            blocks = tp[(i // 2) * sb:(i // 2 + 1) * sb, (i % 2) * c:(i % 2 + 1) * c]
            w[key]["tinv"] = jnp.where(diag_blocks, jnp.tile(blocks, (c // sb, 1)), 0.0)
        return w

    def lower_rows(mat, width):
        return jnp.concatenate([mat[b * 2 * width + width:(b + 1) * 2 * width] for b in range(c // (2 * width))],
                               axis=0)

    def per_key(fn):
        def stage(group, w):
            for key in group:
                fn(w[key])
            return w
        return stage

    def merge_first(mask, width):
        def fn(wk):
            wk["tb"] = wk["tinv"].astype(BF16)
            wk["half"] = _dot(lower_rows(wk["tinv"], width).astype(BF16),
                              jnp.where(mask, wk["l"], 0.0).astype(BF16)).astype(BF16)
        return per_key(fn)

    def merge_second(width):
        def fn(wk):
            t = wk["tinv"]
            new_low = lower_rows(t, width) - _dot(wk["half"], wk["tb"])
            parts = []
            for b in range(c // (2 * width)):
                parts += [t[b * 2 * width:b * 2 * width + width], new_low[b * width:(b + 1) * width]]
            wk["tinv"] = jnp.concatenate(parts, axis=0)
        return per_key(fn)

    def solve(wk):
        wk["uw"] = _dot(wk["tinv"].astype(BF16), wk["rhs"]).astype(BF16)

    def state_free(wk):
        both = _dot(jnp.concatenate([wk["a_intra"], wk["k_dec_t"]], axis=0), wk["uw"])
        wk["a_uw"] = both[:c]
        wk["k_uw"] = both[c:]
        wk["lhs"] = jnp.concatenate([wk["q_dec"] - both[:c, DN_DIM:], both[c:, DN_DIM:]],
                                    axis=0).astype(BF16)

    state = {(bi, h): state_ref[bi, h] for bi in range(nb) for h in heads}

    def recurrence(group, w):
        for slot in sorted({ch for ch, _ in group}):
            recurrence_slot([key for key in group if key[0] == slot], w)

    def recurrence_slot(keys, w):
        prod = {}
        for ch, h in keys:
            prod[ch, h] = _dot(w[ch, h]["lhs"], state[locate(ch)[0], h].astype(BF16))
        for ch, h in keys:
            bi, r0 = locate(ch)
            wk = w[ch, h]
            o = prod[ch, h][:c] + wk["a_uw"][:, :DN_DIM]
            state[bi, h] = state[bi, h] * wk["d_chunk"] - prod[ch, h][c:] + wk["k_uw"][:, :DN_DIM]
            on = o * lax.rsqrt(jnp.mean(o * o, axis=-1, keepdims=True) + NORM_EPS) * gain_ref[...]
            zs = zs_ref[bi, r0:r0 + c, h * DN_DIM:(h + 1) * DN_DIM].astype(F32)
            write_out(bi, r0, h, (on * zs).astype(BF16))

    stages = [load_and_gram, intra_chunk, diag_inverse]
    for level, mask in enumerate(merge_masks):
        stages += [merge_first(mask, sb << level), merge_second(sb << level)]
    stages += [per_key(solve), per_key(state_free), recurrence]
    n_slots = nb * n_chunks
    size = min(DN_GROUP, n_slots)
    groups = [[(ch, h) for ch in range(g, g + size) for h in heads] for g in range(0, n_slots, size)]

    def finalize():
        for key, value in state.items():
            state_ref[key] = value

    return _staged_steps(groups, stages), finalize


def _gdn_spread_matrix():
    sb = DN_SUB
    mat = np.zeros((LANES, (sb - 1) * LANES), np.float32)
    for j in range(sb - 1):
        for m in range(LANES // sb):
            mat[sb * m + j, j * LANES + sb * m:j * LANES + sb * (m + 1)] = 1.0
    return mat


def _mix_kernel(tiles_per_seq, n_tiles,
                x_ref, oa_ref, gt_ref, mod_ref, g2_ref, wb_ref, wo_ref, wgu_ref, wd_ref,
                dqkv_ref, bg_ref, zs_ref, gain_ref, bcat_ref,
                o_ref, state_ref, gc_ref, od_ref):
    i = pl.program_id(0)
    slot_new = i % 2
    slot_prev = (i + 1) % 2
    c = DN_CHUNK
    hid = [(0, FFN_SPLIT), (FFN_SPLIT, FFN_HIDDEN)]

    @pl.when(i == 0)
    def _():
        od_ref[1] = jnp.zeros(od_ref.shape[1:], BF16)

    def store_delta_out(bi, r0, h, value):
        od_ref[slot_new, bi, r0:r0 + c, h * DN_DIM:(h + 1) * DN_DIM] = value

    first_tile = jnp.minimum(i, n_tiles - 1) % tiles_per_seq == 0
    delta_steps, delta_finalize = _gdn_program(first_tile, dqkv_ref, bg_ref, zs_ref, gain_ref, bcat_ref,
                                               store_delta_out, state_ref, gc_ref)

    def branch_proj(s, _):
        return dict(ya=_dot(oa_ref[s], wb_ref[:ATTN_Q_WIDTH, :]),
                    yd=_dot(od_ref[slot_prev, s], wb_ref[ATTN_Q_WIDTH:, :]))

    def merge(s, w):
        sa = _sigmoid_of_twice(gt_ref[s, :, :D_MODEL].astype(F32))
        sd = _sigmoid_of_twice(gt_ref[s, :, D_MODEL:].astype(F32))
        return dict(merged=(sa * w["ya"] + sd * w["yd"]).astype(BF16))

    def out_proj(s, w):
        return dict(out=_dot(w["merged"], wo_ref[...]))

    def residual_norm(s, w):
        x1 = x_ref[s] + mod_ref[s, 2:3, :] * w["out"]
        ms = jnp.mean(x1 * x1, axis=-1, keepdims=True)
        gain_scale2 = g2_ref[...] * (1.0 + mod_ref[s, 4:5, :])
        return dict(x1=x1, h=(x1 * lax.rsqrt(ms + NORM_EPS) * gain_scale2 + mod_ref[s, 3:4, :]).astype(BF16))

    def gate_up(part):
        lo, hi = hid[part]

        def stage(i, w):
            w[f"g{part}"] = _dot(w["h"], wgu_ref[:, lo:hi])
            w[f"u{part}"] = _dot(w["h"], wgu_ref[:, FFN_HIDDEN + lo:FFN_HIDDEN + hi])
            return w
        return stage

    def activation(part):
        def stage(i, w):
            w[f"act{part}"] = (_silu_of_twice(w.pop(f"g{part}")) * w.pop(f"u{part}")).astype(BF16)
            return w
        return stage

    def down_proj(s, w):
        w["y"] = sum(_dot(w[f"act{p}"], wd_ref[lo:hi, :]) for p, (lo, hi) in enumerate(hid))
        return w

    def finish(s, w):
        o_ref[s] = w["x1"] + mod_ref[s, 5:6, :] * w["y"]

    ffn_steps = _staged_steps([0, 1], [branch_proj, merge, out_proj, residual_norm, gate_up(0), activation(0),
                                       gate_up(1), activation(1), down_proj, finish], fine=True)
    _interleave(ffn_steps, delta_steps)
    delta_finalize()


def _mix(x, oa, gt, mod, g2, wb, wo, wgu, wd, dqkv, bg, zs, gain, tt):
    bsz, seq, d = x.shape
    nb = DN_SEQS
    assert bsz % nb == 0, "the fused mixer/FFN kernel takes the sequences in pairs"
    tps = seq // tt
    n = (bsz // nb) * tps
    bcat = jnp.asarray(_gdn_spread_matrix(), BF16)

    def cur(i):
        return jnp.minimum(i, n - 1)

    def prev(i):
        return jnp.maximum(i - 1, 0)

    def tile(width, which):
        return pl.BlockSpec((nb, tt, width), lambda i: (which(i) // tps, which(i) % tps, 0))

    return pl.pallas_call(
        functools.partial(_mix_kernel, tps, n),
        grid=(n + 1,),
        in_specs=[tile(d, prev), tile(ATTN_Q_WIDTH, prev), tile(2 * d, prev),
                  pl.BlockSpec((nb, 6, d), lambda i: (prev(i) // tps, 0, 0)), _const_spec((1, d)),
                  _const_spec(wb.shape), _const_spec(wo.shape), _const_spec(wgu.shape), _const_spec(wd.shape),
                  tile(DN_CONV_DIM, cur), tile(LANES, cur), tile(DN_WIDTH, cur),
                  _const_spec(gain.shape), _const_spec(bcat.shape)],
        out_specs=tile(d, prev),
        out_shape=jax.ShapeDtypeStruct((bsz, seq, d), F32),
        scratch_shapes=[pltpu.VMEM((nb, DN_HEADS, DN_DIM, DN_DIM), F32),
                        pltpu.VMEM((nb, tt, LANES), F32),
                        pltpu.VMEM((2, nb, tt, DN_WIDTH), BF16)],
        compiler_params=_params(1),
        name="mix",
    )(x, oa, gt, mod, g2, wb, wo, wgu, wd, dqkv, bg, zs, gain, bcat)


def _rope_freqs():
    inv_freq = ROPE_THETA ** (-jnp.arange(0, ROT_DIM, 2, dtype=F32) / ROT_DIM)
    return jnp.broadcast_to(inv_freq[:, None], (ROT_DIM // 2, LANES))


def _pick_tile(seq, want):
    tt = min(want, seq)
    assert seq % tt == 0 and tt % ATTN_BLOCK == 0, (seq, tt)
    return tt


def kernel(x, c, positions, ada_w, ada_b, norm1_g, w_in, conv_w, q_norm_g, k_norm_g, sinks, a_log,
           dt_bias, dn_norm_g, w_branch, w_out, norm2_g, w_gate_up, w_down):
    bsz, seq, d = x.shape
    assert d == D_MODEL and ada_w.shape[0] == 1, "single-layer kernel"
    n_in = ATTN_Q_WIDTH + 2 * ATTN_KV_WIDTH + DN_CONV_DIM

    mod = _ada(c, ada_w[0], ada_b[0]).reshape(bsz, 6, d)

    w = w_in[0]
    w_packed = jnp.concatenate(
        [w[:, :n_in], w[:, n_in:n_in + 2 * DN_HEADS],
         jnp.zeros((d, LANES - 2 * DN_HEADS), w.dtype), 0.5 * w[:, n_in + 2 * DN_HEADS:]],
        axis=1).astype(BF16)
    assert w_packed.shape[1] == IN_PACKED
    wgu = jnp.concatenate([0.5 * w_gate_up[0][:, :FFN_HIDDEN], w_gate_up[0][:, FFN_HIDDEN:]],
                          axis=1).astype(BF16)

    qkg = jnp.stack([jnp.tile(q_norm_g[0], ATTN_HEADS),
                     jnp.pad(jnp.tile(k_norm_g[0], ATTN_KV_HEADS), (0, ATTN_Q_WIDTH - ATTN_KV_WIDTH))])
    freq = _rope_freqs()
    convw = 0.5 * conv_w[0].reshape(DN_CONV, DN_CONV_DIM)
    pad4 = (DN_HEADS, LANES - 2 * DN_HEADS)
    alog = jnp.pad(a_log[0], pad4).reshape(1, LANES)
    dtb = jnp.pad(dt_bias[0], pad4).reshape(1, LANES)
    sink_lanes = jnp.repeat(sinks[0], ATTN_HEAD_DIM).reshape(ATTN_HEADS // 2, LANES)

    q, k, v, dqkv, bg, zs, sg = _inproj(
        x, mod, positions.reshape(bsz, 1, seq), norm1_g[0].reshape(1, d), w_packed, qkg, freq,
        convw, alog, dtb, _pick_tile(seq, 256))
    oa = _attn(q, k, v, sink_lanes, _pick_tile(seq, 4096))
    return _mix(x, oa, sg, mod, norm2_g[0].reshape(1, d), w_branch[0].astype(BF16), w_out[0].astype(BF16),
                wgu, w_down[0].astype(BF16), dqkv, bg, zs, dn_norm_g[0].reshape(1, DN_DIM),
                _pick_tile(seq, 256))
```

```python
import functools

import jax
import jax.numpy as jnp
import numpy as np
from jax import lax
from jax.experimental import pallas as pl
from jax.experimental.pallas import tpu as pltpu

F32 = jnp.float32
BF16 = jnp.bfloat16

D_MODEL = 1024
ATTN_HEADS = 8
ATTN_KV_HEADS = 2
ATTN_HEAD_DIM = 64
ATTN_BLOCK = 128
ROT_DIM = ATTN_HEAD_DIM // 4
ROPE_THETA = 500000.0
ATTN_Q_WIDTH = ATTN_HEADS * ATTN_HEAD_DIM
ATTN_KV_WIDTH = ATTN_KV_HEADS * ATTN_HEAD_DIM
DN_HEADS = 4
DN_DIM = 128
DN_CONV = 4
DN_CHUNK = 64
DN_SUB = 8
DN_GROUP = 2
DN_SEQS = 4
DN_WIDTH = DN_HEADS * DN_DIM
DN_CONV_DIM = 3 * DN_WIDTH
FFN_HIDDEN = 2816
FFN_SPLIT = 1536
NORM_EPS = 1e-6
LANES = 128

COL_QKV = 0
COL_DN = COL_QKV + ATTN_Q_WIDTH + 2 * ATTN_KV_WIDTH
COL_BA = COL_DN + DN_CONV_DIM
COL_Z = COL_BA + LANES
COL_GATES = COL_Z + DN_WIDTH
IN_PACKED = COL_GATES + 2 * D_MODEL

VMEM_LIMIT = 56 * 1024 * 1024


def _sigmoid(x):
    return 0.5 * jnp.tanh(0.5 * x) + 0.5


def _sigmoid_of_twice(xh):
    return 0.5 * jnp.tanh(xh) + 0.5


def _silu_of_twice(xh):
    return xh * jnp.tanh(xh) + xh


def _dot(a, b):
    return jnp.dot(a, b, preferred_element_type=F32)


def _dot_nt(a, b):
    return lax.dot_general(a, b, (((1,), (1,)), ((), ())), preferred_element_type=F32)


def _software_pipeline(stages):
    pending = None
    for issue, finish in stages:
        value = issue()
        if pending is not None:
            pending[0](pending[1])
        pending = (finish, value)
    pending[0](pending[1])


def _staged_pipeline(items, stages):
    depth = len(stages)
    values = {}
    for step in range(len(items) + depth - 1):
        for k in range(depth):
            i = step - k
            if 0 <= i < len(items):
                values[i] = stages[k](items[i], values.get(i))
                if k == depth - 1:
                    del values[i]


def _const_spec(shape):
    nd = len(shape)
    return pl.BlockSpec(shape, lambda *_: (0,) * nd, pipeline_mode=pl.Buffered(1))


def _params(n_grid):
    return pltpu.CompilerParams(dimension_semantics=("arbitrary",) * n_grid,
                                vmem_limit_bytes=VMEM_LIMIT)


def _ada_kernel(c_ref, w_ref, b_ref, o_ref):
    c = c_ref[...]
    cond = c * _sigmoid(c)
    o_ref[...] = jnp.dot(cond, w_ref[...], preferred_element_type=F32,
                         precision=lax.Precision.HIGHEST) + b_ref[...]


def _ada(c, w, b):
    bsz, d = c.shape
    n = w.shape[1]
    bn = 1536
    return pl.pallas_call(
        _ada_kernel,
        grid=(n // bn,),
        in_specs=[pl.BlockSpec((bsz, d), lambda j: (0, 0)),
                  pl.BlockSpec((d, bn), lambda j: (0, j)),
                  pl.BlockSpec((1, bn), lambda j: (0, j))],
        out_specs=pl.BlockSpec((bsz, bn), lambda j: (0, j)),
        out_shape=jax.ShapeDtypeStruct((bsz, n), F32),
        compiler_params=_params(1),
        name="ada",
    )(c, w, b.reshape(1, n))


def _inproj_kernel(tiles_per_seq, x_ref, mod_ref, pos_ref, g1_ref, w_ref, qkg_ref, freq_ref,
                   convw_ref, alog_ref, dtb_ref,
                   q_ref, k_ref, v_ref, dqkv_ref, bg_ref, zs_ref, sg_ref,
                   xc_ref, h_ref):
    i = pl.program_id(0)
    tt = x_ref.shape[1]
    t = jnp.maximum(i - 1, 0) % tiles_per_seq
    slot_cur = (i + 1) % 2
    slot_new = i % 2

    @pl.when(i == 0)
    def _():
        h_ref[1] = jnp.zeros(h_ref.shape[1:], BF16)

    h = h_ref[slot_cur]

    def normalise_next_tile():
        x = x_ref[0]
        ms = jnp.mean(x * x, axis=-1, keepdims=True)
        gain_scale = g1_ref[...] * (1.0 + mod_ref[0, 1:2, :])
        h_ref[slot_new] = (x * lax.rsqrt(ms + NORM_EPS) * gain_scale + mod_ref[0, 0:1, :]).astype(BF16)

    @pl.when(t == 0)
    def _():
        xc_ref[0:8, :] = jnp.zeros((8, DN_CONV_DIM), F32)

    ang = jnp.tile(freq_ref[...], (1, tt // LANES)) * pos_ref[0].astype(F32)
    cos_t = jnp.cos(ang)
    sin_t = jnp.sin(ang)
    one_t = jnp.ones_like(cos_t)
    zero_t = jnp.zeros_like(cos_t)

    def lane_table(first, second, fill):
        head = [first, second] + [fill] * (ATTN_HEAD_DIM // 8 - 2)
        return jnp.concatenate(head * (LANES // ATTN_HEAD_DIM), axis=0).T

    cos = lane_table(cos_t, cos_t, one_t)
    sin_signed = lane_table(-sin_t, sin_t, zero_t)
    half = ROT_DIM // 2
    first_half = (lax.broadcasted_iota(jnp.int32, (tt, LANES), 1) % ROT_DIM) < half

    def project(lo, hi):
        return lambda: _dot(h, w_ref[:, lo:hi])

    low_head = lax.broadcasted_iota(jnp.int32, (tt, LANES), 1) < ATTN_HEAD_DIM

    def finish_qkv(qkv):
        def head_rms_norm(u, gain):
            sq = u * u
            total = jnp.sum(sq, axis=-1, keepdims=True)
            low = jnp.sum(jnp.where(low_head, sq, 0.0), axis=-1, keepdims=True)
            mean_sq = jnp.where(low_head, low, total - low) * (1.0 / ATTN_HEAD_DIM)
            return u * lax.rsqrt(mean_sq + NORM_EPS) * gain

        def rope(u):
            partner = jnp.where(first_half, pltpu.roll(u, LANES - half, axis=1), pltpu.roll(u, half, axis=1))
            return u * cos + partner * sin_signed

        for i in range(ATTN_Q_WIDTH // LANES):
            sl = slice(i * LANES, (i + 1) * LANES)
            qn = head_rms_norm(qkv[:, sl], qkg_ref[0:1, sl])
            q_ref[0, :, sl] = (rope(qn) * (ATTN_HEAD_DIM ** -0.5)).astype(BF16)
        kn = head_rms_norm(qkv[:, ATTN_Q_WIDTH:ATTN_Q_WIDTH + ATTN_KV_WIDTH], qkg_ref[1:2, :ATTN_KV_WIDTH])
        k_ref[0] = rope(kn).astype(BF16)
        v_ref[0] = qkv[:, ATTN_Q_WIDTH + ATTN_KV_WIDTH:].astype(BF16)

    def finish_conv(group):
        c0 = group * DN_WIDTH

        def finish(proj):
            cols = slice(c0, c0 + DN_WIDTH)
            carry = xc_ref[:, cols]
            xc_ref[:, cols] = proj[tt - 8:, :]
            row8 = lax.broadcasted_iota(jnp.int32, (8, DN_WIDTH), 0)
            conv = convw_ref[DN_CONV - 1:DN_CONV, cols] * proj
            slabs = [carry] + [proj[8 * g:8 * (g + 1), :] for g in range(tt // 8)]
            for s in range(1, DN_CONV):
                rot = [pltpu.roll(slab, s, axis=0) for slab in slabs]
                shifted = jnp.concatenate(
                    [jnp.where(row8 < s, rot[g], rot[g + 1]) for g in range(tt // 8)], axis=0)
                conv = conv + convw_ref[DN_CONV - 1 - s:DN_CONV - s, cols] * shifted
            act = _silu_of_twice(conv)
            if group == 2:
                dqkv_ref[0, :, cols] = act.astype(BF16)
                return
            scale = DN_DIM ** -0.5 if group == 0 else 1.0
            for i in range(DN_HEADS):
                u = act[:, i * DN_DIM:(i + 1) * DN_DIM]
                un = u * (lax.rsqrt(jnp.sum(u * u, axis=-1, keepdims=True) + NORM_EPS) * scale)
                dqkv_ref[0, :, c0 + i * DN_DIM:c0 + (i + 1) * DN_DIM] = un.astype(BF16)

        return finish

    def finish_ba_z(baz):
        ba = baz[:, :LANES]
        z = baz[:, LANES:]
        sp_in = ba + dtb_ref[...]
        softplus = jnp.maximum(sp_in, 0.0) + jnp.log(1.0 + jnp.exp(-jnp.abs(sp_in)))
        g = -jnp.exp(alog_ref[...]) * softplus
        lane = lax.broadcasted_iota(jnp.int32, ba.shape, 1)
        bg_ref[0] = jnp.where(lane < DN_HEADS, _sigmoid(ba), g)
        zs_ref[0] = _silu_of_twice(z).astype(BF16)

    def finish_gate(i):
        def finish(gate):
            sg_ref[0, :, i * D_MODEL:(i + 1) * D_MODEL] = gate.astype(BF16)
            if i == 0:
                normalise_next_tile()
        return finish

    conv = [(project(COL_DN + g * DN_WIDTH, COL_DN + (g + 1) * DN_WIDTH), finish_conv(g)) for g in range(3)]
    gate = [(project(COL_GATES + i * D_MODEL, COL_GATES + (i + 1) * D_MODEL), finish_gate(i)) for i in range(2)]
    _software_pipeline([(project(COL_QKV, COL_DN), finish_qkv), conv[0], gate[0], conv[1], gate[1], conv[2],
                        (project(COL_BA, COL_GATES), finish_ba_z)])


def _inproj(x, mod, pos, g1, w_packed, qkg, freq, convw, alog, dtb, tt):
    bsz, seq, d = x.shape
    tps = seq // tt
    n = bsz * tps

    def cur(i):
        return jnp.minimum(i, n - 1)

    def prev(i):
        return jnp.maximum(i - 1, 0)

    def tile(width):
        return pl.BlockSpec((1, tt, width), lambda i: (prev(i) // tps, prev(i) % tps, 0))

    out_shapes = [
        jax.ShapeDtypeStruct((bsz, seq, ATTN_Q_WIDTH), BF16),
        jax.ShapeDtypeStruct((bsz, seq, ATTN_KV_WIDTH), BF16),
        jax.ShapeDtypeStruct((bsz, seq, ATTN_KV_WIDTH), BF16),
        jax.ShapeDtypeStruct((bsz, seq, DN_CONV_DIM), BF16),
        jax.ShapeDtypeStruct((bsz, seq, LANES), F32),
        jax.ShapeDtypeStruct((bsz, seq, DN_WIDTH), BF16),
        jax.ShapeDtypeStruct((bsz, seq, 2 * D_MODEL), BF16),
    ]
    return pl.pallas_call(
        functools.partial(_inproj_kernel, tps),
        grid=(n + 1,),
        in_specs=[pl.BlockSpec((1, tt, d), lambda i: (cur(i) // tps, cur(i) % tps, 0)),
                  pl.BlockSpec((1, 6, d), lambda i: (cur(i) // tps, 0, 0)),
                  pl.BlockSpec((1, 1, tt), lambda i: (prev(i) // tps, 0, prev(i) % tps)),
                  _const_spec((1, d)),
                  _const_spec(w_packed.shape),
                  _const_spec(qkg.shape),
                  _const_spec(freq.shape),
                  _const_spec(convw.shape),
                  _const_spec(alog.shape),
                  _const_spec(dtb.shape)],
        out_specs=[tile(ATTN_Q_WIDTH), tile(ATTN_KV_WIDTH), tile(ATTN_KV_WIDTH), tile(DN_CONV_DIM),
                   tile(LANES), tile(DN_WIDTH), tile(2 * D_MODEL)],
        out_shape=out_shapes,
        scratch_shapes=[pltpu.VMEM((8, DN_CONV_DIM), F32),
                        pltpu.VMEM((2, tt, d), BF16)],
        compiler_params=_params(1),
        name="inproj",
    )(x, mod, pos, g1, w_packed, qkg, freq, convw, alog, dtb)


def _attn_kernel(q_ref, k_ref, v_ref, kp_ref, vp_ref, sink_ref, o_ref, kw_ref, vw_ref):
    t = pl.program_id(1)
    tt = q_ref.shape[1]
    blk = ATTN_BLOCK
    hd = ATTN_HEAD_DIM

    kw_ref[blk:, :] = k_ref[0]
    vw_ref[blk:, :] = v_ref[0]

    @pl.when(t == 0)
    def _():
        kw_ref[:blk, :] = jnp.zeros((blk, ATTN_KV_WIDTH), BF16)
        vw_ref[:blk, :] = jnp.zeros((blk, ATTN_KV_WIDTH), BF16)

    @pl.when(t > 0)
    def _():
        kw_ref[:blk, :] = kp_ref[0]
        vw_ref[:blk, :] = vp_ref[0]

    row = lax.broadcasted_iota(jnp.int32, (blk, 2 * blk), 0)
    col = lax.broadcasted_iota(jnp.int32, (blk, 2 * blk), 1)
    band = (col > row) & (col <= row + blk)
    win = tt + blk
    lo_kv = lax.broadcasted_iota(jnp.int32, (win, LANES), 1) < hd
    lo_q = lax.broadcasted_iota(jnp.int32, (blk, LANES), 1) < hd
    ones_lo = jnp.where(lo_kv, 1.0, 0.0).astype(BF16)
    ones_hi = jnp.where(lo_kv, 0.0, 1.0).astype(BF16)

    kwin = kw_ref[...].astype(F32)
    vwin = vw_ref[...].astype(F32)
    kswp = pltpu.roll(kwin, hd, axis=1)
    vswp = pltpu.roll(vwin, hd, axis=1)
    k_lo, k_hi, v_lo, v_hi = [], [], [], []
    for g in range(ATTN_KV_HEADS):
        k_lo.append(jnp.where(lo_kv, kwin if g == 0 else kswp, 0.0).astype(BF16))
        k_hi.append(jnp.where(lo_kv, 0.0, kswp if g == 0 else kwin).astype(BF16))
        v_lo.append(jnp.concatenate(
            [jnp.where(lo_kv, vwin if g == 0 else vswp, 0.0).astype(BF16), ones_lo], axis=1))
        v_hi.append(jnp.concatenate(
            [jnp.where(lo_kv, 0.0, vswp if g == 0 else vwin).astype(BF16), ones_hi], axis=1))

    valid_first = band & (col >= jnp.where(t == 0, blk, 0))
    pairs_per_kv = ATTN_HEADS // ATTN_KV_HEADS // 2
    items = [(j, g) for j in range(tt // blk) for g in range(ATTN_KV_HEADS)]

    def scores(item, _):
        j, g = item
        rows = slice(j * blk, (j + 2) * blk)
        lhs = jnp.concatenate([q_ref[0, j * blk:(j + 1) * blk, p * LANES:(p + 1) * LANES]
                               for p in range(g * pairs_per_kv, (g + 1) * pairs_per_kv)], axis=0)
        return _dot_nt(lhs, jnp.concatenate([k_lo[g][rows], k_hi[g][rows]], axis=0))

    def softmax(item, s):
        j, g = item
        valid = valid_first if j == 0 else band
        probs, m_pairs = [], []
        for i in range(pairs_per_kv):
            p = g * pairs_per_kv + i
            ps, ms = [], []
            for e in range(2):
                se = jnp.where(valid, s[i * blk:(i + 1) * blk, e * 2 * blk:(e + 1) * 2 * blk], -jnp.inf)
                m = jnp.maximum(jnp.max(se, axis=-1, keepdims=True), sink_ref[p:p + 1, e * hd:e * hd + 1])
                ps.append(jnp.exp(se - m).astype(BF16))
                ms.append(m)
            probs.append(jnp.concatenate(ps, axis=1))
            m_pairs.append(jnp.where(lo_q, ms[0], ms[1]))
        return jnp.concatenate(probs, axis=0), m_pairs

    def values(item, soft):
        j, g = item
        rows = slice(j * blk, (j + 2) * blk)
        pv = _dot(soft[0], jnp.concatenate([v_lo[g][rows], v_hi[g][rows]], axis=0))
        return pv, soft[1]

    def finish(item, res):
        j, g = item
        pv, m_pairs = res
        for i in range(pairs_per_kv):
            p = g * pairs_per_kv + i
            denom = pv[i * blk:(i + 1) * blk, LANES:] + jnp.exp(sink_ref[p:p + 1, :] - m_pairs[i])
            o_ref[0, j * blk:(j + 1) * blk, p * LANES:(p + 1) * LANES] = (
                pv[i * blk:(i + 1) * blk, :LANES] / denom).astype(BF16)

    _staged_pipeline(items, [scores, softmax, values, finish])


def _attn(q, k, v, sink_lanes, tt):
    bsz, seq, _ = q.shape
    nprev = tt // ATTN_BLOCK

    def tile(width):
        return pl.BlockSpec((1, tt, width), lambda b, t: (b, t, 0))

    def prev(width):
        return pl.BlockSpec((1, ATTN_BLOCK, width),
                            lambda b, t: (b, jnp.maximum(t * nprev - 1, 0), 0))

    return pl.pallas_call(
        _attn_kernel,
        grid=(bsz, seq // tt),
        in_specs=[tile(ATTN_Q_WIDTH), tile(ATTN_KV_WIDTH), tile(ATTN_KV_WIDTH),
                  prev(ATTN_KV_WIDTH), prev(ATTN_KV_WIDTH),
                  _const_spec(sink_lanes.shape)],
        out_specs=tile(ATTN_Q_WIDTH),
        out_shape=jax.ShapeDtypeStruct((bsz, seq, ATTN_Q_WIDTH), BF16),
        scratch_shapes=[pltpu.VMEM((tt + ATTN_BLOCK, ATTN_KV_WIDTH), BF16),
                        pltpu.VMEM((tt + ATTN_BLOCK, ATTN_KV_WIDTH), BF16)],
        compiler_params=_params(2),
        name="attn",
    )(q, k, v, k, v, sink_lanes)


def _gdn_kernel(dqkv_ref, bg_ref, zs_ref, gain_ref, bcat_ref, o_ref, state_ref, gc_ref):
    t = pl.program_id(1)
    nb, tt = dqkv_ref.shape[0], dqkv_ref.shape[1]
    c = DN_CHUNK
    n_chunks = tt // c

    @pl.when(t == 0)
    def _():
        state_ref[...] = jnp.zeros(state_ref.shape, F32)

    rin = lax.broadcasted_iota(jnp.int32, (tt, LANES), 0) % c
    gcts = []
    for bi in range(nb):
        gc = bg_ref[bi]
        step = 1
        while step < c:
            gc = gc + jnp.where(rin >= step, pltpu.roll(gc, step, axis=0), 0.0)
            step *= 2
        gc_ref[bi] = gc
        gcts.append(gc.T)

    def locate(ch):
        return ch % nb, (ch // nb) * c

    ri = lax.broadcasted_iota(jnp.int32, (c, c), 0)
    ci = lax.broadcasted_iota(jnp.int32, (c, c), 1)
    incl = ri >= ci
    strict = ri > ci
    sb = DN_SUB
    diag_blocks = (ri // sb) == (ci // sb)
    merge_masks = []
    width = sb
    while width < c:
        merge_masks.append(((ri // (2 * width)) == (ci // (2 * width))) & ((ri // width) != (ci // width)))
        width *= 2
    lane_blk = lax.broadcasted_iota(jnp.int32, (sb, c), 1) // sb
    sub_p = lax.broadcasted_iota(jnp.int32, (sb, LANES), 0)
    lane_p = lax.broadcasted_iota(jnp.int32, (sb, LANES), 1)
    eye_packed = jnp.where(sub_p == lane_p % sb, 1.0, 0.0).astype(F32)

    heads = range(DN_HEADS)

    def load_and_gram(group, _):
        w = {}
        for ch, h in group:
            bi, r0 = locate(ch)
            q = dqkv_ref[bi, r0:r0 + c, h * DN_DIM:(h + 1) * DN_DIM].astype(F32)
            k = dqkv_ref[bi, r0:r0 + c, DN_WIDTH + h * DN_DIM:DN_WIDTH + (h + 1) * DN_DIM].astype(F32)
            kb = k * bg_ref[bi, r0:r0 + c, h:h + 1]
            a = _dot_nt(jnp.concatenate([kb, q], axis=0).astype(BF16), k.astype(BF16))
            w[ch, h] = dict(q=q, k=k, kb=kb, a=a)
        return w

    def intra_chunk(group, w):
        packed = []
        for ch, h in group:
            bi, r0 = locate(ch)
            wk = w[ch, h]
            v = dqkv_ref[bi, r0:r0 + c, 2 * DN_WIDTH + h * DN_DIM:2 * DN_WIDTH + (h + 1) * DN_DIM].astype(F32)
            beta = bg_ref[bi, r0:r0 + c, h:h + 1]
            gcc = gc_ref[bi, r0:r0 + c, DN_HEADS + h:DN_HEADS + h + 1]
            gcr = gcts[bi][DN_HEADS + h:DN_HEADS + h + 1, r0:r0 + c]
            g_last = gcc[c - 1:c, :]
            eg = jnp.exp(gcc)
            decay = jnp.exp(jnp.where(incl, gcc - gcr, -jnp.inf))
            l_mat = jnp.where(strict, wk["a"][:c] * decay, 0.0)
            d8 = jnp.zeros((sb, c), F32)
            for g in range(c // sb):
                d8 = d8 + jnp.where(lane_blk == g, l_mat[g * sb:(g + 1) * sb, :], 0.0)
            packed.append(d8)
            w[ch, h] = dict(
                l=l_mat, a_intra=(wk["a"][c:] * decay).astype(BF16),
                rhs=jnp.concatenate([v * beta, wk["kb"] * eg], axis=1).astype(BF16),
                q_dec=wk["q"] * eg, k_dec_t=(wk["k"] * jnp.exp(g_last - gcc)).T.astype(BF16),
                d_chunk=jnp.exp(g_last))
        pall = jnp.concatenate(
            [jnp.concatenate(packed[2 * i:2 * i + 2], axis=1) for i in range(len(packed) // 2)], axis=0)
        w["packed"] = pall
        p_hi = pall.astype(BF16)
        p_lo = (pall - p_hi.astype(F32)).astype(BF16)
        spread = _dot(p_hi, bcat_ref[...]) + _dot(p_lo, bcat_ref[...])
        w["cols"] = [spread[:, j * LANES:(j + 1) * LANES] for j in range(sb - 1)]
        return w

    def diag_inverse(group, w):
        pall = w["packed"]
        n_pairs = pall.shape[0] // sb
        cols = w["cols"]
        tp = jnp.concatenate([eye_packed] * n_pairs, axis=0)
        for j in range(sb - 1):
            row_j = jnp.concatenate(
                [jnp.broadcast_to(tp[p * sb + j:p * sb + j + 1, :], (sb, LANES)) for p in range(n_pairs)], axis=0)
            tp = tp - cols[j] * row_j
        for i, key in enumerate(group):
            blocks = tp[(i // 2) * sb:(i // 2 + 1) * sb, (i % 2) * c:(i % 2 + 1) * c]
            w[key]["tinv"] = jnp.where(diag_blocks, jnp.tile(blocks, (c // sb, 1)), 0.0)
        return w

    def lower_rows(mat, width):
        return jnp.concatenate([mat[b * 2 * width + width:(b + 1) * 2 * width] for b in range(c // (2 * width))],
                               axis=0)

    def per_key(fn):
        def stage(group, w):
            for key in group:
                fn(w[key])
            return w
        return stage

    def merge_first(mask, width):
        def fn(wk):
            wk["tb"] = wk["tinv"].astype(BF16)
            wk["half"] = _dot(lower_rows(wk["tinv"], width).astype(BF16),
                              jnp.where(mask, wk["l"], 0.0).astype(BF16)).astype(BF16)
        return per_key(fn)

    def merge_second(width):
        def fn(wk):
            t = wk["tinv"]
            new_low = lower_rows(t, width) - _dot(wk["half"], wk["tb"])
            parts = []
            for b in range(c // (2 * width)):
                parts += [t[b * 2 * width:b * 2 * width + width], new_low[b * width:(b + 1) * width]]
            wk["tinv"] = jnp.concatenate(parts, axis=0)
        return per_key(fn)

    def solve(wk):
        wk["uw"] = _dot(wk["tinv"].astype(BF16), wk["rhs"]).astype(BF16)

    def state_free(wk):
        both = _dot(jnp.concatenate([wk["a_intra"], wk["k_dec_t"]], axis=0), wk["uw"])
        wk["a_uw"] = both[:c]
        wk["k_uw"] = both[c:]
        wk["lhs"] = jnp.concatenate([wk["q_dec"] - both[:c, DN_DIM:], both[c:, DN_DIM:]],
                                    axis=0).astype(BF16)

    state = {(bi, h): state_ref[bi, h] for bi in range(nb) for h in heads}

    def recurrence(group, w):
        for slot in sorted({ch for ch, _ in group}):
            recurrence_slot([key for key in group if key[0] == slot], w)

    def recurrence_slot(keys, w):
        prod = {}
        for ch, h in keys:
            prod[ch, h] = _dot(w[ch, h]["lhs"], state[locate(ch)[0], h].astype(BF16))
        for ch, h in keys:
            bi, r0 = locate(ch)
            wk = w[ch, h]
            o = prod[ch, h][:c] + wk["a_uw"][:, :DN_DIM]
            state[bi, h] = state[bi, h] * wk["d_chunk"] - prod[ch, h][c:] + wk["k_uw"][:, :DN_DIM]
            on = o * lax.rsqrt(jnp.mean(o * o, axis=-1, keepdims=True) + NORM_EPS) * gain_ref[...]
            zs = zs_ref[bi, r0:r0 + c, h * DN_DIM:(h + 1) * DN_DIM].astype(F32)
            o_ref[bi, r0:r0 + c, h * DN_DIM:(h + 1) * DN_DIM] = (on * zs).astype(BF16)

    stages = [load_and_gram, intra_chunk, diag_inverse]
    for level, mask in enumerate(merge_masks):
        stages += [merge_first(mask, sb << level), merge_second(sb << level)]
    stages += [per_key(solve), per_key(state_free), recurrence]
    n_slots = nb * n_chunks
    size = min(DN_GROUP, n_slots)
    groups = [[(ch, h) for ch in range(g, g + size) for h in heads] for g in range(0, n_slots, size)]
    _staged_pipeline(groups, stages)
    for key, value in state.items():
        state_ref[key] = value


def _gdn_spread_matrix():
    sb = DN_SUB
    mat = np.zeros((LANES, (sb - 1) * LANES), np.float32)
    for j in range(sb - 1):
        for m in range(LANES // sb):
            mat[sb * m + j, j * LANES + sb * m:j * LANES + sb * (m + 1)] = 1.0
    return mat


def _gdn(dqkv, bg, zs, gain, tt):
    bsz, seq, _ = dqkv.shape
    nb = DN_SEQS if bsz % DN_SEQS == 0 else 1

    def tile(width):
        return pl.BlockSpec((nb, tt, width), lambda b, t: (b, t, 0))

    bcat = jnp.asarray(_gdn_spread_matrix(), BF16)
    return pl.pallas_call(
        _gdn_kernel,
        grid=(bsz // nb, seq // tt),
        in_specs=[tile(DN_CONV_DIM), tile(LANES), tile(DN_WIDTH), _const_spec(gain.shape),
                  _const_spec(bcat.shape)],
        out_specs=tile(DN_WIDTH),
        out_shape=jax.ShapeDtypeStruct((bsz, seq, DN_WIDTH), BF16),
        scratch_shapes=[pltpu.VMEM((nb, DN_HEADS, DN_DIM, DN_DIM), F32),
                        pltpu.VMEM((nb, tt, LANES), F32)],
        compiler_params=_params(2),
        name="gdn",
    )(dqkv, bg, zs, gain, bcat)


def _mix_ffn_kernel(x_ref, oa_ref, od_ref, gt_ref, mod_ref, g2_ref, wb_ref, wo_ref, wgu_ref, wd_ref, o_ref):
    tt = x_ref.shape[1]
    half_rows = tt // 2
    gate1 = mod_ref[0, 2:3, :]
    shift2 = mod_ref[0, 3:4, :]
    gain_scale2 = g2_ref[...] * (1.0 + mod_ref[0, 4:5, :])
    gate2 = mod_ref[0, 5:6, :]
    hid = [(0, FFN_SPLIT), (FFN_SPLIT, FFN_HIDDEN)]

    def rows(i):
        return slice(i * half_rows, (i + 1) * half_rows)

    def branch_proj(i, _):
        return dict(ya=_dot(oa_ref[0, rows(i), :], wb_ref[:ATTN_Q_WIDTH, :]),
                    yd=_dot(od_ref[0, rows(i), :], wb_ref[ATTN_Q_WIDTH:, :]))

    def merge(i, w):
        sa = _sigmoid_of_twice(gt_ref[0, rows(i), :D_MODEL].astype(F32))
        sd = _sigmoid_of_twice(gt_ref[0, rows(i), D_MODEL:].astype(F32))
        return dict(merged=(sa * w["ya"] + sd * w["yd"]).astype(BF16))

    def out_proj(i, w):
        return dict(out=_dot(w["merged"], wo_ref[...]))

    def residual_norm(i, w):
        x1 = x_ref[0, rows(i), :] + gate1 * w["out"]
        ms = jnp.mean(x1 * x1, axis=-1, keepdims=True)
        return dict(x1=x1, h=(x1 * lax.rsqrt(ms + NORM_EPS) * gain_scale2 + shift2).astype(BF16))

    def gate_up(part):
        lo, hi = hid[part]

        def stage(i, w):
            w[f"g{part}"] = _dot(w["h"], wgu_ref[:, lo:hi])
            w[f"u{part}"] = _dot(w["h"], wgu_ref[:, FFN_HIDDEN + lo:FFN_HIDDEN + hi])
            return w
        return stage

    def activation(part):
        def stage(i, w):
            w[f"act{part}"] = (_silu_of_twice(w.pop(f"g{part}")) * w.pop(f"u{part}")).astype(BF16)
            return w
        return stage

    def down_proj(i, w):
        w["y"] = sum(_dot(w[f"act{p}"], wd_ref[lo:hi, :]) for p, (lo, hi) in enumerate(hid))
        return w

    def finish(i, w):
        o_ref[0, rows(i), :] = w["x1"] + gate2 * w["y"]

    _staged_pipeline([0, 1], [branch_proj, merge, out_proj, residual_norm, gate_up(0), activation(0),
                              gate_up(1), activation(1), down_proj, finish])


def _mix_ffn(x, oa, od, gt, mod, g2, wb, wo, wgu, wd, tt):
    bsz, seq, d = x.shape

    def tile(width):
        return pl.BlockSpec((1, tt, width), lambda b, t: (b, t, 0))

    return pl.pallas_call(
        _mix_ffn_kernel,
        grid=(bsz, seq // tt),
        in_specs=[tile(d), tile(ATTN_Q_WIDTH), tile(DN_WIDTH), tile(2 * d),
                  pl.BlockSpec((1, 6, d), lambda b, t: (b, 0, 0)), _const_spec((1, d)),
                  _const_spec(wb.shape), _const_spec(wo.shape), _const_spec(wgu.shape), _const_spec(wd.shape)],
        out_specs=tile(d),
        out_shape=jax.ShapeDtypeStruct((bsz, seq, d), F32),
        compiler_params=_params(2),
        name="mix_ffn",
    )(x, oa, od, gt, mod, g2, wb, wo, wgu, wd)


def _rope_freqs():
    inv_freq = ROPE_THETA ** (-jnp.arange(0, ROT_DIM, 2, dtype=F32) / ROT_DIM)
    return jnp.broadcast_to(inv_freq[:, None], (ROT_DIM // 2, LANES))


def _pick_tile(seq, want):
    tt = min(want, seq)
    assert seq % tt == 0 and tt % ATTN_BLOCK == 0, (seq, tt)
    return tt


def kernel(x, c, positions, ada_w, ada_b, norm1_g, w_in, conv_w, q_norm_g, k_norm_g, sinks, a_log,
           dt_bias, dn_norm_g, w_branch, w_out, norm2_g, w_gate_up, w_down):
    bsz, seq, d = x.shape
    assert d == D_MODEL and ada_w.shape[0] == 1, "single-layer kernel"
    n_in = ATTN_Q_WIDTH + 2 * ATTN_KV_WIDTH + DN_CONV_DIM

    mod = _ada(c, ada_w[0], ada_b[0]).reshape(bsz, 6, d)

    w = w_in[0]
    w_packed = jnp.concatenate(
        [w[:, :n_in], w[:, n_in:n_in + 2 * DN_HEADS],
         jnp.zeros((d, LANES - 2 * DN_HEADS), w.dtype), 0.5 * w[:, n_in + 2 * DN_HEADS:]],
        axis=1).astype(BF16)
    assert w_packed.shape[1] == IN_PACKED
    wgu = jnp.concatenate([0.5 * w_gate_up[0][:, :FFN_HIDDEN], w_gate_up[0][:, FFN_HIDDEN:]],
                          axis=1).astype(BF16)

    qkg = jnp.stack([jnp.tile(q_norm_g[0], ATTN_HEADS),
                     jnp.pad(jnp.tile(k_norm_g[0], ATTN_KV_HEADS), (0, ATTN_Q_WIDTH - ATTN_KV_WIDTH))])
    freq = _rope_freqs()
    convw = 0.5 * conv_w[0].reshape(DN_CONV, DN_CONV_DIM)
    pad4 = (DN_HEADS, LANES - 2 * DN_HEADS)
    alog = jnp.pad(a_log[0], pad4).reshape(1, LANES)
    dtb = jnp.pad(dt_bias[0], pad4).reshape(1, LANES)
    sink_lanes = jnp.repeat(sinks[0], ATTN_HEAD_DIM).reshape(ATTN_HEADS // 2, LANES)

    q, k, v, dqkv, bg, zs, sg = _inproj(
        x, mod, positions.reshape(bsz, 1, seq), norm1_g[0].reshape(1, d), w_packed, qkg, freq,
        convw, alog, dtb, _pick_tile(seq, 256))
    oa = _attn(q, k, v, sink_lanes, _pick_tile(seq, 4096))
    od = _gdn(dqkv, bg, zs, dn_norm_g[0].reshape(1, DN_DIM), _pick_tile(seq, 256))
    return _mix_ffn(x, oa, od, sg, mod, norm2_g[0].reshape(1, d), w_branch[0].astype(BF16),
                    w_out[0].astype(BF16), wgu, w_down[0].astype(BF16), _pick_tile(seq, 512))
```

```python
import functools

import jax
import jax.numpy as jnp
import numpy as np
from jax import lax
from jax.experimental import pallas as pl
from jax.experimental.pallas import tpu as pltpu

F32 = jnp.float32
BF16 = jnp.bfloat16

D_MODEL = 1024
ATTN_HEADS = 8
ATTN_KV_HEADS = 2
ATTN_HEAD_DIM = 64
ATTN_BLOCK = 128
ROT_DIM = ATTN_HEAD_DIM // 4
ROPE_THETA = 500000.0
ATTN_Q_WIDTH = ATTN_HEADS * ATTN_HEAD_DIM
ATTN_KV_WIDTH = ATTN_KV_HEADS * ATTN_HEAD_DIM
DN_HEADS = 4
DN_DIM = 128
DN_CONV = 4
DN_CHUNK = 64
DN_SUB = 8
DN_GROUP = 4
DN_SEQS = 16
DN_WIDTH = DN_HEADS * DN_DIM
DN_CONV_DIM = 3 * DN_WIDTH
FFN_HIDDEN = 2816
FFN_SPLIT = 1536
NORM_EPS = 1e-6
LANES = 128

COL_QKV = 0
COL_DN = COL_QKV + ATTN_Q_WIDTH + 2 * ATTN_KV_WIDTH
COL_BA = COL_DN + DN_CONV_DIM
COL_Z = COL_BA + LANES
COL_GATES = COL_Z + DN_WIDTH
IN_PACKED = COL_GATES + 2 * D_MODEL

VMEM_LIMIT = 56 * 1024 * 1024


def _sigmoid(x):
    return 0.5 * jnp.tanh(0.5 * x) + 0.5


def _sigmoid_of_twice(xh):
    return 0.5 * jnp.tanh(xh) + 0.5


def _silu_of_twice(xh):
    return xh * jnp.tanh(xh) + xh


def _dot(a, b):
    return jnp.dot(a, b, preferred_element_type=F32)


def _dot_nt(a, b):
    return lax.dot_general(a, b, (((1,), (1,)), ((), ())), preferred_element_type=F32)


def _software_pipeline(stages):
    pending = None
    for issue, finish in stages:
        value = issue()
        if pending is not None:
            pending[0](pending[1])
        pending = (finish, value)
    pending[0](pending[1])


def _staged_pipeline(items, stages):
    depth = len(stages)
    values = {}
    for step in range(len(items) + depth - 1):
        for k in range(depth):
            i = step - k
            if 0 <= i < len(items):
                values[i] = stages[k](items[i], values.get(i))
                if k == depth - 1:
                    del values[i]


def _const_spec(shape):
    nd = len(shape)
    return pl.BlockSpec(shape, lambda *_: (0,) * nd, pipeline_mode=pl.Buffered(1))


def _params(n_grid):
    return pltpu.CompilerParams(dimension_semantics=("arbitrary",) * n_grid,
                                vmem_limit_bytes=VMEM_LIMIT)


def _ada_kernel(c_ref, w_ref, b_ref, o_ref):
    c = c_ref[...]
    cond = c * _sigmoid(c)
    o_ref[...] = jnp.dot(cond, w_ref[...], preferred_element_type=F32,
                         precision=lax.Precision.HIGHEST) + b_ref[...]


def _ada(c, w, b):
    bsz, d = c.shape
    n = w.shape[1]
    bn = 1536
    return pl.pallas_call(
        _ada_kernel,
        grid=(n // bn,),
        in_specs=[pl.BlockSpec((bsz, d), lambda j: (0, 0)),
                  pl.BlockSpec((d, bn), lambda j: (0, j)),
                  pl.BlockSpec((1, bn), lambda j: (0, j))],
        out_specs=pl.BlockSpec((bsz, bn), lambda j: (0, j)),
        out_shape=jax.ShapeDtypeStruct((bsz, n), F32),
        compiler_params=_params(1),
        name="ada",
    )(c, w, b.reshape(1, n))


def _inproj_kernel(tiles_per_seq, x_ref, mod_ref, pos_ref, g1_ref, w_ref, qkg_ref, freq_ref,
                   convw_ref, alog_ref, dtb_ref,
                   q_ref, k_ref, v_ref, dqkv_ref, bg_ref, zs_ref, sg_ref,
                   xc_ref, h_ref):
    i = pl.program_id(0)
    tt = x_ref.shape[1]
    t = jnp.maximum(i - 1, 0) % tiles_per_seq
    slot_cur = (i + 1) % 2
    slot_new = i % 2

    @pl.when(i == 0)
    def _():
        h_ref[1] = jnp.zeros(h_ref.shape[1:], BF16)

    h = h_ref[slot_cur]

    def normalise_next_tile():
        x = x_ref[0]
        ms = jnp.mean(x * x, axis=-1, keepdims=True)
        gain_scale = g1_ref[...] * (1.0 + mod_ref[0, 1:2, :])
        h_ref[slot_new] = (x * lax.rsqrt(ms + NORM_EPS) * gain_scale + mod_ref[0, 0:1, :]).astype(BF16)

    @pl.when(t == 0)
    def _():
        xc_ref[0:8, :] = jnp.zeros((8, DN_CONV_DIM), F32)

    ang = jnp.tile(freq_ref[...], (1, tt // LANES)) * pos_ref[0].astype(F32)
    cos_t = jnp.cos(ang)
    sin_t = jnp.sin(ang)
    one_t = jnp.ones_like(cos_t)
    zero_t = jnp.zeros_like(cos_t)

    def lane_table(first, second, fill):
        head = [first, second] + [fill] * (ATTN_HEAD_DIM // 8 - 2)
        return jnp.concatenate(head * (LANES // ATTN_HEAD_DIM), axis=0).T

    cos = lane_table(cos_t, cos_t, one_t)
    sin_signed = lane_table(-sin_t, sin_t, zero_t)
    half = ROT_DIM // 2
    first_half = (lax.broadcasted_iota(jnp.int32, (tt, LANES), 1) % ROT_DIM) < half

    def project(lo, hi):
        return lambda: _dot(h, w_ref[:, lo:hi])

    low_head = lax.broadcasted_iota(jnp.int32, (tt, LANES), 1) < ATTN_HEAD_DIM

    def finish_qkv(qkv):
        def head_rms_norm(u, gain):
            sq = u * u
            total = jnp.sum(sq, axis=-1, keepdims=True)
            low = jnp.sum(jnp.where(low_head, sq, 0.0), axis=-1, keepdims=True)
            mean_sq = jnp.where(low_head, low, total - low) * (1.0 / ATTN_HEAD_DIM)
            return u * lax.rsqrt(mean_sq + NORM_EPS) * gain

        def rope(u):
            partner = jnp.where(first_half, pltpu.roll(u, LANES - half, axis=1), pltpu.roll(u, half, axis=1))
            return u * cos + partner * sin_signed

        for i in range(ATTN_Q_WIDTH // LANES):
            sl = slice(i * LANES, (i + 1) * LANES)
            qn = head_rms_norm(qkv[:, sl], qkg_ref[0:1, sl])
            q_ref[0, :, sl] = (rope(qn) * (ATTN_HEAD_DIM ** -0.5)).astype(BF16)
        kn = head_rms_norm(qkv[:, ATTN_Q_WIDTH:ATTN_Q_WIDTH + ATTN_KV_WIDTH], qkg_ref[1:2, :ATTN_KV_WIDTH])
        k_ref[0] = rope(kn).astype(BF16)
        v_ref[0] = qkv[:, ATTN_Q_WIDTH + ATTN_KV_WIDTH:].astype(BF16)

    def finish_conv(group):
        c0 = group * DN_WIDTH

        def finish(proj):
            cols = slice(c0, c0 + DN_WIDTH)
            carry = xc_ref[:, cols]
            xc_ref[:, cols] = proj[tt - 8:, :]
            row8 = lax.broadcasted_iota(jnp.int32, (8, DN_WIDTH), 0)
            conv = convw_ref[DN_CONV - 1:DN_CONV, cols] * proj
            slabs = [carry] + [proj[8 * g:8 * (g + 1), :] for g in range(tt // 8)]
            for s in range(1, DN_CONV):
                rot = [pltpu.roll(slab, s, axis=0) for slab in slabs]
                shifted = jnp.concatenate(
                    [jnp.where(row8 < s, rot[g], rot[g + 1]) for g in range(tt // 8)], axis=0)
                conv = conv + convw_ref[DN_CONV - 1 - s:DN_CONV - s, cols] * shifted
            act = _silu_of_twice(conv)
            if group == 2:
                dqkv_ref[0, :, cols] = act.astype(BF16)
                return
            scale = DN_DIM ** -0.5 if group == 0 else 1.0
            for i in range(DN_HEADS):
                u = act[:, i * DN_DIM:(i + 1) * DN_DIM]
                un = u * (lax.rsqrt(jnp.sum(u * u, axis=-1, keepdims=True) + NORM_EPS) * scale)
                dqkv_ref[0, :, c0 + i * DN_DIM:c0 + (i + 1) * DN_DIM] = un.astype(BF16)

        return finish

    def finish_ba_z(baz):
        ba = baz[:, :LANES]
        z = baz[:, LANES:]
        sp_in = ba + dtb_ref[...]
        softplus = jnp.maximum(sp_in, 0.0) + jnp.log(1.0 + jnp.exp(-jnp.abs(sp_in)))
        g = -jnp.exp(alog_ref[...]) * softplus
        lane = lax.broadcasted_iota(jnp.int32, ba.shape, 1)
        bg_ref[0] = jnp.where(lane < DN_HEADS, _sigmoid(ba), g)
        zs_ref[0] = _silu_of_twice(z).astype(BF16)

    def finish_gate(i):
        def finish(gate):
            sg_ref[0, :, i * D_MODEL:(i + 1) * D_MODEL] = gate.astype(BF16)
            if i == 0:
                normalise_next_tile()
        return finish

    conv = [(project(COL_DN + g * DN_WIDTH, COL_DN + (g + 1) * DN_WIDTH), finish_conv(g)) for g in range(3)]
    gate = [(project(COL_GATES + i * D_MODEL, COL_GATES + (i + 1) * D_MODEL), finish_gate(i)) for i in range(2)]
    _software_pipeline([(project(COL_QKV, COL_DN), finish_qkv), conv[0], gate[0], conv[1], gate[1], conv[2],
                        (project(COL_BA, COL_GATES), finish_ba_z)])


def _inproj(x, mod, pos, g1, w_packed, qkg, freq, convw, alog, dtb, tt):
    bsz, seq, d = x.shape
    tps = seq // tt
    n = bsz * tps

    def cur(i):
        return jnp.minimum(i, n - 1)

    def prev(i):
        return jnp.maximum(i - 1, 0)

    def tile(width):
        return pl.BlockSpec((1, tt, width), lambda i: (prev(i) // tps, prev(i) % tps, 0))

    out_shapes = [
        jax.ShapeDtypeStruct((bsz, seq, ATTN_Q_WIDTH), BF16),
        jax.ShapeDtypeStruct((bsz, seq, ATTN_KV_WIDTH), BF16),
        jax.ShapeDtypeStruct((bsz, seq, ATTN_KV_WIDTH), BF16),
        jax.ShapeDtypeStruct((bsz, seq, DN_CONV_DIM), BF16),
        jax.ShapeDtypeStruct((bsz, seq, LANES), F32),
        jax.ShapeDtypeStruct((bsz, seq, DN_WIDTH), BF16),
        jax.ShapeDtypeStruct((bsz, seq, 2 * D_MODEL), BF16),
    ]
    return pl.pallas_call(
        functools.partial(_inproj_kernel, tps),
        grid=(n + 1,),
        in_specs=[pl.BlockSpec((1, tt, d), lambda i: (cur(i) // tps, cur(i) % tps, 0)),
                  pl.BlockSpec((1, 6, d), lambda i: (cur(i) // tps, 0, 0)),
                  pl.BlockSpec((1, 1, tt), lambda i: (prev(i) // tps, 0, prev(i) % tps)),
                  _const_spec((1, d)),
                  _const_spec(w_packed.shape),
                  _const_spec(qkg.shape),
                  _const_spec(freq.shape),
                  _const_spec(convw.shape),
                  _const_spec(alog.shape),
                  _const_spec(dtb.shape)],
        out_specs=[tile(ATTN_Q_WIDTH), tile(ATTN_KV_WIDTH), tile(ATTN_KV_WIDTH), tile(DN_CONV_DIM),
                   tile(LANES), tile(DN_WIDTH), tile(2 * D_MODEL)],
        out_shape=out_shapes,
        scratch_shapes=[pltpu.VMEM((8, DN_CONV_DIM), F32),
                        pltpu.VMEM((2, tt, d), BF16)],
        compiler_params=_params(1),
        name="inproj",
    )(x, mod, pos, g1, w_packed, qkg, freq, convw, alog, dtb)


def _attn_kernel(q_ref, k_ref, v_ref, kp_ref, vp_ref, sink_ref, o_ref, kw_ref, vw_ref):
    t = pl.program_id(1)
    tt = q_ref.shape[1]
    blk = ATTN_BLOCK
    hd = ATTN_HEAD_DIM

    kw_ref[blk:, :] = k_ref[0]
    vw_ref[blk:, :] = v_ref[0]

    @pl.when(t == 0)
    def _():
        kw_ref[:blk, :] = jnp.zeros((blk, ATTN_KV_WIDTH), BF16)
        vw_ref[:blk, :] = jnp.zeros((blk, ATTN_KV_WIDTH), BF16)

    @pl.when(t > 0)
    def _():
        kw_ref[:blk, :] = kp_ref[0]
        vw_ref[:blk, :] = vp_ref[0]

    row = lax.broadcasted_iota(jnp.int32, (blk, 2 * blk), 0)
    col = lax.broadcasted_iota(jnp.int32, (blk, 2 * blk), 1)
    band = (col > row) & (col <= row + blk)
    win = tt + blk
    lo_kv = lax.broadcasted_iota(jnp.int32, (win, LANES), 1) < hd
    lo_q = lax.broadcasted_iota(jnp.int32, (blk, LANES), 1) < hd
    ones_lo = jnp.where(lo_kv, 1.0, 0.0).astype(BF16)
    ones_hi = jnp.where(lo_kv, 0.0, 1.0).astype(BF16)

    kwin = kw_ref[...].astype(F32)
    vwin = vw_ref[...].astype(F32)
    kswp = pltpu.roll(kwin, hd, axis=1)
    vswp = pltpu.roll(vwin, hd, axis=1)
    k_lo, k_hi, v_lo, v_hi = [], [], [], []
    for g in range(ATTN_KV_HEADS):
        k_lo.append(jnp.where(lo_kv, kwin if g == 0 else kswp, 0.0).astype(BF16))
        k_hi.append(jnp.where(lo_kv, 0.0, kswp if g == 0 else kwin).astype(BF16))
        v_lo.append(jnp.concatenate(
            [jnp.where(lo_kv, vwin if g == 0 else vswp, 0.0).astype(BF16), ones_lo], axis=1))
        v_hi.append(jnp.concatenate(
            [jnp.where(lo_kv, 0.0, vswp if g == 0 else vwin).astype(BF16), ones_hi], axis=1))

    valid_first = band & (col >= jnp.where(t == 0, blk, 0))
    pairs_per_kv = ATTN_HEADS // ATTN_KV_HEADS // 2
    items = [(j, g) for j in range(tt // blk) for g in range(ATTN_KV_HEADS)]

    def scores(item, _):
        j, g = item
        rows = slice(j * blk, (j + 2) * blk)
        lhs = jnp.concatenate([q_ref[0, j * blk:(j + 1) * blk, p * LANES:(p + 1) * LANES]
                               for p in range(g * pairs_per_kv, (g + 1) * pairs_per_kv)], axis=0)
        return _dot_nt(lhs, jnp.concatenate([k_lo[g][rows], k_hi[g][rows]], axis=0))

    def softmax(item, s):
        j, g = item
        valid = valid_first if j == 0 else band
        probs, m_pairs = [], []
        for i in range(pairs_per_kv):
            p = g * pairs_per_kv + i
            ps, ms = [], []
            for e in range(2):
                se = jnp.where(valid, s[i * blk:(i + 1) * blk, e * 2 * blk:(e + 1) * 2 * blk], -jnp.inf)
                m = jnp.maximum(jnp.max(se, axis=-1, keepdims=True), sink_ref[p:p + 1, e * hd:e * hd + 1])
                ps.append(jnp.exp(se - m).astype(BF16))
                ms.append(m)
            probs.append(jnp.concatenate(ps, axis=1))
            m_pairs.append(jnp.where(lo_q, ms[0], ms[1]))
        return jnp.concatenate(probs, axis=0), m_pairs

    def values(item, soft):
        j, g = item
        rows = slice(j * blk, (j + 2) * blk)
        pv = _dot(soft[0], jnp.concatenate([v_lo[g][rows], v_hi[g][rows]], axis=0))
        return pv, soft[1]

    def finish(item, res):
        j, g = item
        pv, m_pairs = res
        for i in range(pairs_per_kv):
            p = g * pairs_per_kv + i
            denom = pv[i * blk:(i + 1) * blk, LANES:] + jnp.exp(sink_ref[p:p + 1, :] - m_pairs[i])
            o_ref[0, j * blk:(j + 1) * blk, p * LANES:(p + 1) * LANES] = (
                pv[i * blk:(i + 1) * blk, :LANES] / denom).astype(BF16)

    _staged_pipeline(items, [scores, softmax, values, finish])


def _attn(q, k, v, sink_lanes, tt):
    bsz, seq, _ = q.shape
    nprev = tt // ATTN_BLOCK

    def tile(width):
        return pl.BlockSpec((1, tt, width), lambda b, t: (b, t, 0))

    def prev(width):
        return pl.BlockSpec((1, ATTN_BLOCK, width),
                            lambda b, t: (b, jnp.maximum(t * nprev - 1, 0), 0))

    return pl.pallas_call(
        _attn_kernel,
        grid=(bsz, seq // tt),
        in_specs=[tile(ATTN_Q_WIDTH), tile(ATTN_KV_WIDTH), tile(ATTN_KV_WIDTH),
                  prev(ATTN_KV_WIDTH), prev(ATTN_KV_WIDTH),
                  _const_spec(sink_lanes.shape)],
        out_specs=tile(ATTN_Q_WIDTH),
        out_shape=jax.ShapeDtypeStruct((bsz, seq, ATTN_Q_WIDTH), BF16),
        scratch_shapes=[pltpu.VMEM((tt + ATTN_BLOCK, ATTN_KV_WIDTH), BF16),
                        pltpu.VMEM((tt + ATTN_BLOCK, ATTN_KV_WIDTH), BF16)],
        compiler_params=_params(2),
        name="attn",
    )(q, k, v, k, v, sink_lanes)


def _gdn_kernel(dqkv_ref, bg_ref, zs_ref, gain_ref, bcat_ref, o_ref, state_ref, gc_ref):
    t = pl.program_id(1)
    nb, tt = dqkv_ref.shape[0], dqkv_ref.shape[1]
    c = DN_CHUNK
    n_chunks = tt // c

    @pl.when(t == 0)
    def _():
        state_ref[...] = jnp.zeros(state_ref.shape, F32)

    rin = lax.broadcasted_iota(jnp.int32, (tt, LANES), 0) % c
    gcts = []
    for bi in range(nb):
        gc = bg_ref[bi]
        step = 1
        while step < c:
            gc = gc + jnp.where(rin >= step, pltpu.roll(gc, step, axis=0), 0.0)
            step *= 2
        gc_ref[bi] = gc
        gcts.append(gc.T)

    def locate(ch):
        return ch % nb, (ch // nb) * c

    ri = lax.broadcasted_iota(jnp.int32, (c, c), 0)
    ci = lax.broadcasted_iota(jnp.int32, (c, c), 1)
    incl = ri >= ci
    strict = ri > ci
    sb = DN_SUB
    diag_blocks = (ri // sb) == (ci // sb)
    merge_masks = []
    width = sb
    while width < c:
        merge_masks.append(((ri // (2 * width)) == (ci // (2 * width))) & ((ri // width) != (ci // width)))
        width *= 2
    lane_blk = lax.broadcasted_iota(jnp.int32, (sb, c), 1) // sb
    sub_p = lax.broadcasted_iota(jnp.int32, (sb, LANES), 0)
    lane_p = lax.broadcasted_iota(jnp.int32, (sb, LANES), 1)
    eye_packed = jnp.where(sub_p == lane_p % sb, 1.0, 0.0).astype(F32)

    heads = range(DN_HEADS)

    def load_and_gram(group, _):
        w = {}
        for ch, h in group:
            bi, r0 = locate(ch)
            q = dqkv_ref[bi, r0:r0 + c, h * DN_DIM:(h + 1) * DN_DIM].astype(F32)
            k = dqkv_ref[bi, r0:r0 + c, DN_WIDTH + h * DN_DIM:DN_WIDTH + (h + 1) * DN_DIM].astype(F32)
            kb = k * bg_ref[bi, r0:r0 + c, h:h + 1]
            a = _dot_nt(jnp.concatenate([kb, q], axis=0).astype(BF16), k.astype(BF16))
            w[ch, h] = dict(q=q, k=k, kb=kb, a=a)
        return w

    def intra_chunk(group, w):
        packed = []
        for ch, h in group:
            bi, r0 = locate(ch)
            wk = w[ch, h]
            v = dqkv_ref[bi, r0:r0 + c, 2 * DN_WIDTH + h * DN_DIM:2 * DN_WIDTH + (h + 1) * DN_DIM].astype(F32)
            beta = bg_ref[bi, r0:r0 + c, h:h + 1]
            gcc = gc_ref[bi, r0:r0 + c, DN_HEADS + h:DN_HEADS + h + 1]
            gcr = gcts[bi][DN_HEADS + h:DN_HEADS + h + 1, r0:r0 + c]
            g_last = gcc[c - 1:c, :]
            eg = jnp.exp(gcc)
            decay = jnp.exp(jnp.where(incl, gcc - gcr, -jnp.inf))
            l_mat = jnp.where(strict, wk["a"][:c] * decay, 0.0)
            d8 = jnp.zeros((sb, c), F32)
            for g in range(c // sb):
                d8 = d8 + jnp.where(lane_blk == g, l_mat[g * sb:(g + 1) * sb, :], 0.0)
            packed.append(d8)
            w[ch, h] = dict(
                l=l_mat, a_intra=(wk["a"][c:] * decay).astype(BF16),
                rhs=jnp.concatenate([v * beta, wk["kb"] * eg], axis=1).astype(BF16),
                q_dec=wk["q"] * eg, k_dec_t=(wk["k"] * jnp.exp(g_last - gcc)).T.astype(BF16),
                d_chunk=jnp.exp(g_last))
        pall = jnp.concatenate(
            [jnp.concatenate(packed[2 * i:2 * i + 2], axis=1) for i in range(len(packed) // 2)], axis=0)
        w["packed"] = pall
        p_hi = pall.astype(BF16)
        p_lo = (pall - p_hi.astype(F32)).astype(BF16)
        spread = _dot(p_hi, bcat_ref[...]) + _dot(p_lo, bcat_ref[...])
        w["cols"] = [spread[:, j * LANES:(j + 1) * LANES] for j in range(sb - 1)]
        return w

    def diag_inverse(group, w):
        pall = w["packed"]
        n_pairs = pall.shape[0] // sb
        cols = w["cols"]
        tp = jnp.concatenate([eye_packed] * n_pairs, axis=0)
        for j in range(sb - 1):
            row_j = jnp.concatenate(
                [jnp.broadcast_to(tp[p * sb + j:p * sb + j + 1, :], (sb, LANES)) for p in range(n_pairs)], axis=0)
            tp = tp - cols[j] * row_j
        for i, key in enumerate(group):
            blocks = tp[(i // 2) * sb:(i // 2 + 1) * sb, (i % 2) * c:(i % 2 + 1) * c]
            w[key]["tinv"] = jnp.where(diag_blocks, jnp.tile(blocks, (c // sb, 1)), 0.0)
        return w

    def lower_rows(mat, width):
        return jnp.concatenate([mat[b * 2 * width + width:(b + 1) * 2 * width] for b in range(c // (2 * width))],
                               axis=0)

    def per_key(fn):
        def stage(group, w):
            for key in group:
                fn(w[key])
            return w
        return stage

    def merge_first(mask, width):
        def fn(wk):
            wk["tb"] = wk["tinv"].astype(BF16)
            wk["half"] = _dot(lower_rows(wk["tinv"], width).astype(BF16),
                              jnp.where(mask, wk["l"], 0.0).astype(BF16)).astype(BF16)
        return per_key(fn)

    def merge_second(width):
        def fn(wk):
            t = wk["tinv"]
            new_low = lower_rows(t, width) - _dot(wk["half"], wk["tb"])
            parts = []
            for b in range(c // (2 * width)):
                parts += [t[b * 2 * width:b * 2 * width + width], new_low[b * width:(b + 1) * width]]
            wk["tinv"] = jnp.concatenate(parts, axis=0)
        return per_key(fn)

    def solve(wk):
        wk["uw"] = _dot(wk["tinv"].astype(BF16), wk["rhs"]).astype(BF16)

    def state_free(wk):
        both = _dot(jnp.concatenate([wk["a_intra"], wk["k_dec_t"]], axis=0), wk["uw"])
        wk["a_uw"] = both[:c]
        wk["k_uw"] = both[c:]
        wk["lhs"] = jnp.concatenate([wk["q_dec"] - both[:c, DN_DIM:], both[c:, DN_DIM:]],
                                    axis=0).astype(BF16)

    state = {(bi, h): state_ref[bi, h] for bi in range(nb) for h in heads}

    def recurrence(group, w):
        for slot in sorted({ch for ch, _ in group}):
            recurrence_slot([key for key in group if key[0] == slot], w)

    def recurrence_slot(keys, w):
        prod = {}
        for ch, h in keys:
            prod[ch, h] = _dot(w[ch, h]["lhs"], state[locate(ch)[0], h].astype(BF16))
        for ch, h in keys:
            bi, r0 = locate(ch)
            wk = w[ch, h]
            o = prod[ch, h][:c] + wk["a_uw"][:, :DN_DIM]
            state[bi, h] = state[bi, h] * wk["d_chunk"] - prod[ch, h][c:] + wk["k_uw"][:, :DN_DIM]
            on = o * lax.rsqrt(jnp.mean(o * o, axis=-1, keepdims=True) + NORM_EPS) * gain_ref[...]
            zs = zs_ref[bi, r0:r0 + c, h * DN_DIM:(h + 1) * DN_DIM].astype(F32)
            o_ref[bi, r0:r0 + c, h * DN_DIM:(h + 1) * DN_DIM] = (on * zs).astype(BF16)

    stages = [load_and_gram, intra_chunk, diag_inverse]
    for level, mask in enumerate(merge_masks):
        stages += [merge_first(mask, sb << level), merge_second(sb << level)]
    stages += [per_key(solve), per_key(state_free), recurrence]
    n_slots = nb * n_chunks
    size = min(DN_GROUP, n_slots)
    groups = [[(ch, h) for ch in range(g, g + size) for h in heads] for g in range(0, n_slots, size)]
    _staged_pipeline(groups, stages)
    for key, value in state.items():
        state_ref[key] = value


def _gdn_spread_matrix():
    sb = DN_SUB
    mat = np.zeros((LANES, (sb - 1) * LANES), np.float32)
    for j in range(sb - 1):
        for m in range(LANES // sb):
            mat[sb * m + j, j * LANES + sb * m:j * LANES + sb * (m + 1)] = 1.0
    return mat


def _gdn(dqkv, bg, zs, gain, tt):
    bsz, seq, _ = dqkv.shape
    nb = DN_SEQS if bsz % DN_SEQS == 0 else 1

    def tile(width):
        return pl.BlockSpec((nb, tt, width), lambda b, t: (b, t, 0))

    bcat = jnp.asarray(_gdn_spread_matrix(), BF16)
    return pl.pallas_call(
        _gdn_kernel,
        grid=(bsz // nb, seq // tt),
        in_specs=[tile(DN_CONV_DIM), tile(LANES), tile(DN_WIDTH), _const_spec(gain.shape),
                  _const_spec(bcat.shape)],
        out_specs=tile(DN_WIDTH),
        out_shape=jax.ShapeDtypeStruct((bsz, seq, DN_WIDTH), BF16),
        scratch_shapes=[pltpu.VMEM((nb, DN_HEADS, DN_DIM, DN_DIM), F32),
                        pltpu.VMEM((nb, tt, LANES), F32)],
        compiler_params=_params(2),
        name="gdn",
    )(dqkv, bg, zs, gain, bcat)


def _mix_ffn_kernel(x_ref, oa_ref, od_ref, gt_ref, mod_ref, g2_ref, wb_ref, wo_ref, wgu_ref, wd_ref, o_ref):
    tt = x_ref.shape[1]
    half_rows = tt // 2
    gate1 = mod_ref[0, 2:3, :]
    shift2 = mod_ref[0, 3:4, :]
    gain_scale2 = g2_ref[...] * (1.0 + mod_ref[0, 4:5, :])
    gate2 = mod_ref[0, 5:6, :]
    hid = [(0, FFN_SPLIT), (FFN_SPLIT, FFN_HIDDEN)]

    def rows(i):
        return slice(i * half_rows, (i + 1) * half_rows)

    def branch_proj(i, _):
        return dict(ya=_dot(oa_ref[0, rows(i), :], wb_ref[:ATTN_Q_WIDTH, :]),
                    yd=_dot(od_ref[0, rows(i), :], wb_ref[ATTN_Q_WIDTH:, :]))

    def merge(i, w):
        sa = _sigmoid_of_twice(gt_ref[0, rows(i), :D_MODEL].astype(F32))
        sd = _sigmoid_of_twice(gt_ref[0, rows(i), D_MODEL:].astype(F32))
        return dict(merged=(sa * w["ya"] + sd * w["yd"]).astype(BF16))

    def out_proj(i, w):
        return dict(out=_dot(w["merged"], wo_ref[...]))

    def residual_norm(i, w):
        x1 = x_ref[0, rows(i), :] + gate1 * w["out"]
        ms = jnp.mean(x1 * x1, axis=-1, keepdims=True)
        return dict(x1=x1, h=(x1 * lax.rsqrt(ms + NORM_EPS) * gain_scale2 + shift2).astype(BF16))

    def gate_up(part):
        lo, hi = hid[part]

        def stage(i, w):
            w[f"g{part}"] = _dot(w["h"], wgu_ref[:, lo:hi])
            w[f"u{part}"] = _dot(w["h"], wgu_ref[:, FFN_HIDDEN + lo:FFN_HIDDEN + hi])
            return w
        return stage

    def activation(part):
        def stage(i, w):
            w[f"act{part}"] = (_silu_of_twice(w.pop(f"g{part}")) * w.pop(f"u{part}")).astype(BF16)
            return w
        return stage

    def down_proj(i, w):
        w["y"] = sum(_dot(w[f"act{p}"], wd_ref[lo:hi, :]) for p, (lo, hi) in enumerate(hid))
        return w

    def finish(i, w):
        o_ref[0, rows(i), :] = w["x1"] + gate2 * w["y"]

    _staged_pipeline([0, 1], [branch_proj, merge, out_proj, residual_norm, gate_up(0), activation(0),
                              gate_up(1), activation(1), down_proj, finish])


def _mix_ffn(x, oa, od, gt, mod, g2, wb, wo, wgu, wd, tt):
    bsz, seq, d = x.shape

    def tile(width):
        return pl.BlockSpec((1, tt, width), lambda b, t: (b, t, 0))

    return pl.pallas_call(
        _mix_ffn_kernel,
        grid=(bsz, seq // tt),
        in_specs=[tile(d), tile(ATTN_Q_WIDTH), tile(DN_WIDTH), tile(2 * d),
                  pl.BlockSpec((1, 6, d), lambda b, t: (b, 0, 0)), _const_spec((1, d)),
                  _const_spec(wb.shape), _const_spec(wo.shape), _const_spec(wgu.shape), _const_spec(wd.shape)],
        out_specs=tile(d),
        out_shape=jax.ShapeDtypeStruct((bsz, seq, d), F32),
        compiler_params=_params(2),
        name="mix_ffn",
    )(x, oa, od, gt, mod, g2, wb, wo, wgu, wd)


def _rope_freqs():
    inv_freq = ROPE_THETA ** (-jnp.arange(0, ROT_DIM, 2, dtype=F32) / ROT_DIM)
    return jnp.broadcast_to(inv_freq[:, None], (ROT_DIM // 2, LANES))


def _pick_tile(seq, want):
    tt = min(want, seq)
    assert seq % tt == 0 and tt % ATTN_BLOCK == 0, (seq, tt)
    return tt


def kernel(x, c, positions, ada_w, ada_b, norm1_g, w_in, conv_w, q_norm_g, k_norm_g, sinks, a_log,
           dt_bias, dn_norm_g, w_branch, w_out, norm2_g, w_gate_up, w_down):
    bsz, seq, d = x.shape
    assert d == D_MODEL and ada_w.shape[0] == 1, "single-layer kernel"
    n_in = ATTN_Q_WIDTH + 2 * ATTN_KV_WIDTH + DN_CONV_DIM

    mod = _ada(c, ada_w[0], ada_b[0]).reshape(bsz, 6, d)

    w = w_in[0]
    w_packed = jnp.concatenate(
        [w[:, :n_in], w[:, n_in:n_in + 2 * DN_HEADS],
         jnp.zeros((d, LANES - 2 * DN_HEADS), w.dtype), 0.5 * w[:, n_in + 2 * DN_HEADS:]],
        axis=1).astype(BF16)
    assert w_packed.shape[1] == IN_PACKED
    wgu = jnp.concatenate([0.5 * w_gate_up[0][:, :FFN_HIDDEN], w_gate_up[0][:, FFN_HIDDEN:]],
                          axis=1).astype(BF16)

    qkg = jnp.stack([jnp.tile(q_norm_g[0], ATTN_HEADS),
                     jnp.pad(jnp.tile(k_norm_g[0], ATTN_KV_HEADS), (0, ATTN_Q_WIDTH - ATTN_KV_WIDTH))])
    freq = _rope_freqs()
    convw = 0.5 * conv_w[0].reshape(DN_CONV, DN_CONV_DIM)
    pad4 = (DN_HEADS, LANES - 2 * DN_HEADS)
    alog = jnp.pad(a_log[0], pad4).reshape(1, LANES)
    dtb = jnp.pad(dt_bias[0], pad4).reshape(1, LANES)
    sink_lanes = jnp.repeat(sinks[0], ATTN_HEAD_DIM).reshape(ATTN_HEADS // 2, LANES)

    q, k, v, dqkv, bg, zs, sg = _inproj(
        x, mod, positions.reshape(bsz, 1, seq), norm1_g[0].reshape(1, d), w_packed, qkg, freq,
        convw, alog, dtb, _pick_tile(seq, 256))
    oa = _attn(q, k, v, sink_lanes, _pick_tile(seq, 4096))
    od = _gdn(dqkv, bg, zs, dn_norm_g[0].reshape(1, DN_DIM), DN_CHUNK)
    return _mix_ffn(x, oa, od, sg, mod, norm2_g[0].reshape(1, d), w_branch[0].astype(BF16),
                    w_out[0].astype(BF16), wgu, w_down[0].astype(BF16), _pick_tile(seq, 512))
```
